```python
import math
import jax, jax.numpy as jnp
from jax import lax
import numpy as np

D_MODEL = 1024
BATCH = 32
SEQ = 2048
DEPTH = 1
DEC_BATCH = 8
DEC_SEQ = 16
PAST_LEN = 4096

CHUNK = 64
D_FF = 2816
SSD_EXPAND = 2
D_INNER = SSD_EXPAND * D_MODEL
SSD_HEAD_DIM = 64
SSD_HEADS = D_INNER // SSD_HEAD_DIM
SSD_GROUPS = 8
SSD_HPG = SSD_HEADS // SSD_GROUPS
SSD_STATE = 128
CONV_K = 4
CONV_DIM = D_INNER + 2 * SSD_GROUPS * SSD_STATE
POOL_WINDOWS = (2, 4, 8, 16)
POOL_GROUPS = 4
D_POOL = D_MODEL
POOL_GROUP_DIM = D_POOL // POOL_GROUPS
POOL_BUF = max(POOL_WINDOWS) - 1
N_BRANCHES = 2
D_IN_PROJ = D_INNER + CONV_DIM + SSD_HEADS + D_POOL + N_BRANCHES * D_MODEL
EPS = 1e-6

kernel_name = 'streaming_ssd_pool_hybrid_step'


def _rms_norm(x, g):
    xf = x.astype(jnp.float32)
    y = xf * lax.rsqrt(jnp.mean(xf * xf, axis=-1, keepdims=True) + EPS)
    return (y * g.astype(jnp.float32)).astype(x.dtype)


def _grouped_rms_norm(x, g):
    b, t, _ = x.shape
    xf = x.astype(jnp.float32).reshape(b, t, SSD_GROUPS, D_INNER // SSD_GROUPS)
    y = xf * lax.rsqrt(jnp.mean(xf * xf, axis=-1, keepdims=True) + EPS)
    return (y.reshape(b, t, D_INNER) * g.astype(jnp.float32)).astype(x.dtype)


def _swiglu(x, w_gate, w_up, w_down):
    return (jax.nn.silu(x @ w_gate) * (x @ w_up)) @ w_down


def _causal_dwconv(x, buf, w, b):
    t = x.shape[1]
    xp = jnp.concatenate([buf.astype(x.dtype), x], axis=1)
    out = b.astype(x.dtype)
    for k in range(CONV_K):
        out = out + xp[:, k:k + t] * w[k]
    return out, xp[:, -(CONV_K - 1):]


def _ssd_scan(xh, dt, a, bm, cm, h0):
    b, t = xh.shape[0], xh.shape[1]
    blk = min(CHUNK, t)
    nc = t // blk
    xg = xh.reshape(b, nc, blk, SSD_GROUPS, SSD_HPG, SSD_HEAD_DIM)
    dtg = dt.reshape(b, nc, blk, SSD_GROUPS, SSD_HPG)
    bc = bm.reshape(b, nc, blk, SSD_GROUPS, SSD_STATE)
    cc = cm.reshape(b, nc, blk, SSD_GROUPS, SSD_STATE)
    a_cs = jnp.cumsum(dtg * a.reshape(SSD_GROUPS, SSD_HPG), axis=2)
    xdt = xg * dtg[..., None]
    seg = a_cs[:, :, :, None] - a_cs[:, :, None, :]
    causal = jnp.tril(jnp.ones((blk, blk), dtype=bool))[:, :, None, None]
    decay = jnp.exp(jnp.where(causal, seg, -jnp.inf))
    scores = jnp.einsum('bclgn,bcsgn->bclsg', cc, bc)
    y_diag = jnp.einsum('bclsg,bclsgh,bcsghp->bclghp', scores, decay, xdt)
    decay_to_end = jnp.exp(a_cs[:, :, -1:] - a_cs)
    blk_states = jnp.einsum('bclgn,bclgh,bclghp->bcghpn', bc, decay_to_end, xdt)
    blk_decay = jnp.exp(a_cs[:, :, -1])

    def step(h, inp):
        st, dec = inp
        return dec[..., None, None] * h + st, h

    h0g = h0.reshape(b, SSD_GROUPS, SSD_HPG, SSD_HEAD_DIM, SSD_STATE)
    h_fin, h_prev = lax.scan(step, h0g, (jnp.moveaxis(blk_states, 1, 0), jnp.moveaxis(blk_decay, 1, 0)))
    h_prev = jnp.moveaxis(h_prev, 0, 1)
    y_off = jnp.einsum('bclgn,bclgh,bcghpn->bclghp', cc, jnp.exp(a_cs), h_prev)
    y = (y_diag + y_off).reshape(b, t, SSD_HEADS, SSD_HEAD_DIM)
    return y, h_fin.reshape(b, SSD_HEADS, SSD_HEAD_DIM, SSD_STATE)


def _multiscale_pool(u, buf, pos0):
    t = u.shape[1]
    up = jnp.concatenate([buf.astype(u.dtype), u], axis=1)
    cs = jnp.cumsum(up.astype(jnp.float32), axis=1)
    cs = jnp.pad(cs, ((0, 0), (1, 0), (0, 0)))
    pos = pos0 + jnp.arange(t, dtype=jnp.int32)
    outs = []
    for gi, k in enumerate(POOL_WINDOWS):
        sl = slice(gi * POOL_GROUP_DIM, (gi + 1) * POOL_GROUP_DIM)
        lo = POOL_BUF + 1 - k
        wsum = cs[:, POOL_BUF + 1:POOL_BUF + 1 + t, sl] - cs[:, lo:lo + t, sl]
        cnt = jnp.minimum(pos + 1, k).astype(jnp.float32)[None, :, None]
        outs.append(wsum / cnt)
    pooled = jnp.concatenate(outs, axis=-1).astype(u.dtype) - u
    return pooled, up[:, -POOL_BUF:]


def _mixer(h, conv_buf, ssm_state, pool_buf, pos0, p):
    b, t, _ = h.shape
    proj = h @ p['w_in']
    cuts = np.cumsum([D_INNER, CONV_DIM, SSD_HEADS, D_POOL, D_MODEL]).tolist()
    z, xbc, dt_raw, u_pool, g_a, g_b = jnp.split(proj, cuts, axis=-1)
    xbc, new_conv = _causal_dwconv(xbc, conv_buf, p['conv_w'], p['conv_b'])
    xbc = jax.nn.silu(xbc)
    xs, bm, cm = jnp.split(xbc, [D_INNER, D_INNER + SSD_GROUPS * SSD_STATE], axis=-1)
    xh = xs.reshape(b, t, SSD_HEADS, SSD_HEAD_DIM).astype(jnp.float32)
    dt = jax.nn.softplus(dt_raw.astype(jnp.float32) + p['dt_bias'].astype(jnp.float32))
    a = -jnp.exp(p['a_log'].astype(jnp.float32))
    y, new_ssm = _ssd_scan(xh, dt, a,
                           bm.reshape(b, t, SSD_GROUPS, SSD_STATE).astype(jnp.float32),
                           cm.reshape(b, t, SSD_GROUPS, SSD_STATE).astype(jnp.float32),
                           ssm_state.astype(jnp.float32))
    y = y + p['d_skip'].astype(jnp.float32)[:, None] * xh
    y = y.reshape(b, t, D_INNER).astype(h.dtype) * jax.nn.silu(z)
    y_a = _grouped_rms_norm(y, p['ssd_norm_g']) @ p['w_proj_ssd']
    pooled, new_pool = _multiscale_pool(u_pool, pool_buf, pos0)
    mixed = jnp.einsum('btgc,gcd->btgd', pooled.reshape(b, t, POOL_GROUPS, POOL_GROUP_DIM), p['pool_mix'])
    y_b = (mixed.reshape(b, t, D_POOL) * p['pool_scale']) @ p['w_proj_pool']
    gate_a = jax.nn.sigmoid(g_a.astype(jnp.float32)).astype(h.dtype)
    gate_b = jax.nn.sigmoid(g_b.astype(jnp.float32)).astype(h.dtype)
    out = (gate_a * y_a + gate_b * y_b) @ p['w_out']
    return out, new_conv, new_ssm.astype(ssm_state.dtype), new_pool


def _layer(x, conv_buf, ssm_state, pool_buf, pos0, p):
    f1 = _swiglu(_rms_norm(x, p['ffn1_pre_g']), p['ffn1_w_gate'], p['ffn1_w_up'], p['ffn1_w_down'])
    x = x + 0.5 * _rms_norm(f1, p['ffn1_post_g'])
    m, new_conv, new_ssm, new_pool = _mixer(_rms_norm(x, p['mix_pre_g']), conv_buf, ssm_state, pool_buf, pos0, p)
    x = x + _rms_norm(m, p['mix_post_g'])
    f2 = _swiglu(_rms_norm(x, p['ffn2_pre_g']), p['ffn2_w_gate'], p['ffn2_w_up'], p['ffn2_w_down'])
    x = x + 0.5 * _rms_norm(f2, p['ffn2_post_g'])
    return x, new_conv, new_ssm, new_pool


def setup_inputs(seed: int = 0) -> dict:
    key = jax.random.key(seed)
    ks = jax.random.split(key, 40)

    def nrm(k, shape, scale=1.0):
        return jax.random.normal(k, shape, jnp.float32) * scale

    def gain(k, n):
        return 1.0 + 0.02 * jax.random.normal(k, (DEPTH, n), jnp.float32)

    dt0 = jnp.exp(jax.random.uniform(ks[20], (DEPTH, SSD_HEADS), jnp.float32, math.log(1e-3), math.log(1e-1)))
    return {
        'x_prompt': nrm(ks[0], (BATCH, SEQ, D_MODEL)),
        'x_sample': nrm(ks[1], (DEC_BATCH, DEC_SEQ, D_MODEL)),
        'cache_conv': nrm(ks[2], (DEPTH, DEC_BATCH, CONV_K - 1, CONV_DIM)),
        'state_ssm': nrm(ks[3], (DEPTH, DEC_BATCH, SSD_HEADS, SSD_HEAD_DIM, SSD_STATE), 0.1),
        'cache_pool': nrm(ks[4], (DEPTH, DEC_BATCH, POOL_BUF, D_POOL)),
        'ffn1_pre_g': gain(ks[5], D_MODEL),
        'ffn1_post_g': gain(ks[6], D_MODEL),
        'ffn1_w_gate': nrm(ks[7], (DEPTH, D_MODEL, D_FF), D_MODEL ** -0.5),
        'ffn1_w_up': nrm(ks[8], (DEPTH, D_MODEL, D_FF), D_MODEL ** -0.5),
        'ffn1_w_down': nrm(ks[9], (DEPTH, D_FF, D_MODEL), D_FF ** -0.5),
        'mix_pre_g': gain(ks[10], D_MODEL),
        'mix_post_g': gain(ks[11], D_MODEL),
        'w_in': nrm(ks[12], (DEPTH, D_MODEL, D_IN_PROJ), D_MODEL ** -0.5),
        'conv_w': nrm(ks[13], (DEPTH, CONV_K, CONV_DIM), CONV_K ** -0.5),
        'conv_b': nrm(ks[14], (DEPTH, CONV_DIM), 0.02),
        'dt_bias': dt0 + jnp.log(-jnp.expm1(-dt0)),
        'a_log': jnp.log(jax.random.uniform(ks[15], (DEPTH, SSD_HEADS), jnp.float32, 1.0, 16.0)),
        'd_skip': 1.0 + 0.1 * jax.random.normal(ks[16], (DEPTH, SSD_HEADS), jnp.float32),
        'ssd_norm_g': gain(ks[17], D_INNER),
        'w_proj_ssd': nrm(ks[18], (DEPTH, D_INNER, D_MODEL), D_INNER ** -0.5),
        'pool_mix': nrm(ks[19], (DEPTH, POOL_GROUPS, POOL_GROUP_DIM, POOL_GROUP_DIM), POOL_GROUP_DIM ** -0.5),
        'pool_scale': 1.0 + 0.1 * jax.random.normal(ks[21], (DEPTH, D_POOL), jnp.float32),
        'w_proj_pool': nrm(ks[22], (DEPTH, D_POOL, D_MODEL), D_POOL ** -0.5),
        'w_out': nrm(ks[23], (DEPTH, D_MODEL, D_MODEL), D_MODEL ** -0.5),
        'ffn2_pre_g': gain(ks[24], D_MODEL),
        'ffn2_post_g': gain(ks[25], D_MODEL),
        'ffn2_w_gate': nrm(ks[26], (DEPTH, D_MODEL, D_FF), D_MODEL ** -0.5),
        'ffn2_w_up': nrm(ks[27], (DEPTH, D_MODEL, D_FF), D_MODEL ** -0.5),
        'ffn2_w_down': nrm(ks[28], (DEPTH, D_FF, D_MODEL), D_FF ** -0.5),
    }


def reference(x_prompt, x_sample, cache_conv, state_ssm, cache_pool,
              ffn1_pre_g, ffn1_post_g, ffn1_w_gate, ffn1_w_up, ffn1_w_down,
              mix_pre_g, mix_post_g, w_in, conv_w, conv_b, dt_bias, a_log, d_skip,
              ssd_norm_g, w_proj_ssd, pool_mix, pool_scale, w_proj_pool, w_out,
              ffn2_pre_g, ffn2_post_g, ffn2_w_gate, ffn2_w_up, ffn2_w_down):
    y_p, y_s = x_prompt, x_sample
    bp = x_prompt.shape[0]
    conv_p, ssm_p, pool_p, conv_s, ssm_s, pool_s = [], [], [], [], [], []
    for i in range(DEPTH):
        p = {
            'ffn1_pre_g': ffn1_pre_g[i], 'ffn1_post_g': ffn1_post_g[i],
            'ffn1_w_gate': ffn1_w_gate[i], 'ffn1_w_up': ffn1_w_up[i], 'ffn1_w_down': ffn1_w_down[i],
            'mix_pre_g': mix_pre_g[i], 'mix_post_g': mix_post_g[i], 'w_in': w_in[i],
            'conv_w': conv_w[i], 'conv_b': conv_b[i], 'dt_bias': dt_bias[i], 'a_log': a_log[i],
            'd_skip': d_skip[i], 'ssd_norm_g': ssd_norm_g[i], 'w_proj_ssd': w_proj_ssd[i],
            'pool_mix': pool_mix[i], 'pool_scale': pool_scale[i], 'w_proj_pool': w_proj_pool[i],
            'w_out': w_out[i],
            'ffn2_pre_g': ffn2_pre_g[i], 'ffn2_post_g': ffn2_post_g[i],
            'ffn2_w_gate': ffn2_w_gate[i], 'ffn2_w_up': ffn2_w_up[i], 'ffn2_w_down': ffn2_w_down[i],
        }
        zc = jnp.zeros((bp, CONV_K - 1, CONV_DIM), x_prompt.dtype)
        zs = jnp.zeros((bp, SSD_HEADS, SSD_HEAD_DIM, SSD_STATE), x_prompt.dtype)
        zp = jnp.zeros((bp, POOL_BUF, D_POOL), x_prompt.dtype)
        y_p, c1, s1, q1 = _layer(y_p, zc, zs, zp, 0, p)
        y_s, c2, s2, q2 = _layer(y_s, cache_conv[i], state_ssm[i], cache_pool[i], PAST_LEN, p)
        conv_p.append(c1); ssm_p.append(s1); pool_p.append(q1)
        conv_s.append(c2); ssm_s.append(s2); pool_s.append(q2)
    new_conv_prompt = jnp.stack(conv_p)
    new_ssm_prompt = jnp.stack(ssm_p)
    new_pool_prompt = jnp.stack(pool_p)
    new_conv_sample = jnp.stack(conv_s)
    new_ssm_sample = jnp.stack(ssm_s)
    new_pool_sample = jnp.stack(pool_s)
    return (y_p, y_s, new_conv_prompt, new_ssm_prompt, new_pool_prompt, new_conv_sample, new_ssm_sample, new_pool_sample)
```

```python
import functools

import numpy as np
import jax
import jax.numpy as jnp
from jax import lax
from jax.experimental import pallas as pl
from jax.experimental.pallas import tpu as pltpu

F32 = jnp.float32
BF16 = jnp.bfloat16

D_MODEL = 1024
D_FF = 2816
D_INNER = 2048
HEADS = 32
HEAD_DIM = 64
GROUPS = 8
HPG = HEADS // GROUPS
GROUP_W = HPG * HEAD_DIM
STATE = 128
CONV_K = 4
CONV_DIM = D_INNER + 2 * GROUPS * STATE
D_POOL = 1024
POOL_WINDOWS = (2, 4, 8, 16)
POOL_GROUP_DIM = D_POOL // len(POOL_WINDOWS)
POOL_BUF = max(POOL_WINDOWS) - 1
PAST_LEN = 4096
EPS = 1e-6
CHUNK = 64

LANE = 128
DT_REP = 3
CONV_PAD = 8
POOL_PAD = 16
VMEM_LIMIT = 56 * 1024 * 1024


def _const_spec(shape):
    nd = len(shape)
    return pl.BlockSpec(shape, lambda *_: (0,) * nd, pipeline_mode=pl.Buffered(1))


def _rms(x, g):
    ms = jnp.mean(x * x, axis=-1, keepdims=True)
    return x * lax.rsqrt(ms + EPS) * g


def _dot(a, b):
    return jnp.dot(a, b, preferred_element_type=F32)


def _silu(x):
    return x * jax.nn.sigmoid(x)


def _ffn_kernel(x_ref, pre_ref, wg_ref, wu_ref, wd_ref, post_ref, o_ref):
    x = x_ref[...]
    h = _rms(x, pre_ref[...]).astype(BF16)
    g = _dot(h, wg_ref[...])
    u = _dot(h, wu_ref[...])
    a = (_silu(g) * u).astype(BF16)
    f = _dot(a, wd_ref[...])
    o_ref[...] = x + 0.5 * _rms(f, post_ref[...])


def _ffn(x, pre_g, wg, wu, wd, post_g, tm):
    n = x.shape[0]
    row = pl.BlockSpec((tm, D_MODEL), lambda i: (i, 0))
    return pl.pallas_call(
        _ffn_kernel,
        grid=(n // tm,),
        in_specs=[row, _const_spec((1, D_MODEL)), _const_spec((D_MODEL, D_FF)), _const_spec((D_MODEL, D_FF)),
                  _const_spec((D_FF, D_MODEL)), _const_spec((1, D_MODEL))],
        out_specs=row,
        out_shape=jax.ShapeDtypeStruct((n, D_MODEL), F32),
        compiler_params=pltpu.CompilerParams(dimension_semantics=("parallel",), vmem_limit_bytes=VMEM_LIMIT),
        name="ffn",
    )(x, pre_g, wg, wu, wd, post_g)


def _inproj_kernel(x_ref, g_ref, wz_ref, wxbc_ref, wdt_ref, wu_ref, wga_ref, wgb_ref,
                   z_ref, xbc_ref, dt_ref, u_ref, ga_ref, gb_ref):
    h = _rms(x_ref[...], g_ref[...]).astype(BF16)
    z_ref[...] = _dot(h, wz_ref[...])
    xbc_ref[...] = _dot(h, wxbc_ref[...])
    dt_ref[...] = _dot(h, wdt_ref[...])
    u_ref[...] = _dot(h, wu_ref[...])
    ga_ref[...] = _dot(h, wga_ref[...])
    gb_ref[...] = _dot(h, wgb_ref[...])


def _inproj(x, g, wz, wxbc, wdt, wu, wga, wgb, tm):
    n = x.shape[0]
    widths = (D_INNER, CONV_DIM, LANE, D_POOL, D_MODEL, D_MODEL)

    def row(w):
        return pl.BlockSpec((tm, w), lambda i: (i, 0))

    return pl.pallas_call(
        _inproj_kernel,
        grid=(n // tm,),
        in_specs=[row(D_MODEL), _const_spec((1, D_MODEL))] + [_const_spec((D_MODEL, w)) for w in widths],
        out_specs=[row(w) for w in widths],
        out_shape=[jax.ShapeDtypeStruct((n, w), F32) for w in widths],
        compiler_params=pltpu.CompilerParams(dimension_semantics=("parallel",), vmem_limit_bytes=VMEM_LIMIT),
        name="inproj",
    )(x, g, wz, wxbc, wdt, wu, wga, wgb)


def _expand_heads(v3, e3):
    hi = v3.astype(BF16).astype(F32)
    r1 = v3 - hi
    mid = r1.astype(BF16).astype(F32)
    lo = r1 - mid
    lane = lax.broadcasted_iota(jnp.int32, v3.shape, 1)
    packed = jnp.where(lane < HEADS, hi, jnp.where(lane < 2 * HEADS, mid, jnp.where(lane < 3 * HEADS, lo, 0.0)))
    return _dot(packed.astype(BF16), e3)


def _ssd_kernel(xbc_ref, dtr_ref, z_ref, u_ref, convw_ref, convb_ref, dtb_ref, alog_ref, dskip_ref, ng_ref, e3_ref,
                *rest, nt, valid, pos0, has_init):
    if has_init:
        conv0_ref, ssm0_ref, pool0_ref = rest[:3]
        rest = rest[3:]
    yn_ref, pooled_ref, nconv_ref, nssm_ref, npool_ref, xpad, upad, hst = rest
    t_len = CHUNK
    t = pl.program_id(1)

    @pl.when(t == 0)
    def _():
        if has_init:
            xpad[0:CONV_PAD, :] = conv0_ref[...]
            upad[0:POOL_PAD, :] = pool0_ref[...]
            for g in range(GROUPS):
                hst[g] = ssm0_ref[g * GROUP_W:(g + 1) * GROUP_W, :].T
        else:
            xpad[0:CONV_PAD, :] = jnp.zeros((CONV_PAD, CONV_DIM), F32)
            upad[0:POOL_PAD, :] = jnp.zeros((POOL_PAD, D_POOL), F32)
            hst[...] = jnp.zeros(hst.shape, F32)

    xpad[CONV_PAD:CONV_PAD + t_len, :] = xbc_ref[...]
    conv = convb_ref[...]
    for k in range(CONV_K):
        start = CONV_PAD - (CONV_K - 1) + k
        conv = conv + xpad[start:start + t_len, :] * convw_ref[k:k + 1, :]
    xc = _silu(conv)
    xs = xc[:, :D_INNER]
    bm = xc[:, D_INNER:D_INNER + GROUPS * STATE]
    cm = xc[:, D_INNER + GROUPS * STATE:]

    v = dtr_ref[...] + dtb_ref[...]
    dt3 = jnp.maximum(v, 0.0) + jnp.log1p(jnp.exp(-jnp.abs(v)))
    row = lax.broadcasted_iota(jnp.int32, (t_len, LANE), 0)
    if valid < t_len:
        dt3 = jnp.where(row < valid, dt3, 0.0)
    a3 = -jnp.exp(alog_ref[...])
    col = lax.broadcasted_iota(jnp.int32, (t_len, t_len), 1)
    tril = (col <= lax.broadcasted_iota(jnp.int32, (t_len, t_len), 0)).astype(F32)
    acs3 = jnp.dot(tril, dt3 * a3, precision=lax.Precision.HIGHEST, preferred_element_type=F32)
    e3 = e3_ref[...]
    dt_e = _expand_heads(dt3, e3)
    acs_e = _expand_heads(acs3, e3)

    li = lax.broadcasted_iota(jnp.int32, (t_len, GROUP_W), 0)
    lane = lax.broadcasted_iota(jnp.int32, (t_len, GROUP_W), 1)
    s_of_lane = lane % HEAD_DIM
    diag = li == s_of_lane
    causal = li >= s_of_lane
    bd_rows = lax.broadcasted_iota(jnp.int32, (HPG * t_len, GROUP_W), 0) // t_len
    bd_cols = lax.broadcasted_iota(jnp.int32, (HPG * t_len, GROUP_W), 1) // HEAD_DIM
    blockdiag = bd_rows == bd_cols

    z = z_ref[...]
    for g in range(GROUPS):
        ch = slice(g * GROUP_W, (g + 1) * GROUP_W)
        st = slice(g * STATE, (g + 1) * STATE)
        ae = acs_e[:, ch]
        acs_row = jnp.sum(jnp.where(diag, ae, 0.0), axis=0, keepdims=True)
        decay = jnp.exp(jnp.where(causal, ae - acs_row, -jnp.inf))
        bg = bm[:, st].astype(BF16)
        cg = cm[:, st].astype(BF16)
        scores = lax.dot_general(cg, jnp.concatenate([bg] * HPG, axis=0), (((1,), (1,)), ((), ())),
                                 preferred_element_type=F32)
        xg = xs[:, ch]
        xdt = xg * dt_e[:, ch]
        xdt_bd = jnp.where(blockdiag, jnp.concatenate([xdt.astype(BF16)] * HPG, axis=0), 0.0).astype(BF16)
        y_diag = _dot((scores * decay).astype(BF16), xdt_bd)
        h_prev = hst[g]
        y_off = _dot(cg, h_prev.astype(BF16)) * jnp.exp(ae)
        last = ae[t_len - 1:t_len, :]
        xw = (xdt * jnp.exp(last - ae)).astype(BF16)
        hst[g] = jnp.exp(last) * h_prev + lax.dot_general(bg, xw, (((0,), (0,)), ((), ())),
                                                          preferred_element_type=F32)
        y = (y_diag + y_off + dskip_ref[:, ch] * xg) * _silu(z[:, ch])
        ms = jnp.mean(y * y, axis=-1, keepdims=True)
        yn_ref[:, ch] = (y * lax.rsqrt(ms + EPS) * ng_ref[:, ch]).astype(BF16)

    u = u_ref[...]
    upad[POOL_PAD:POOL_PAD + t_len, :] = u
    pos = pos0 + t * t_len + lax.broadcasted_iota(jnp.int32, (t_len, 1), 0)
    for gi, k in enumerate(POOL_WINDOWS):
        ch = slice(gi * POOL_GROUP_DIM, (gi + 1) * POOL_GROUP_DIM)
        wsum = upad[POOL_PAD - (k - 1):POOL_PAD - (k - 1) + t_len, ch]
        for j in range(k - 2, -1, -1):
            wsum = wsum + upad[POOL_PAD - j:POOL_PAD - j + t_len, ch]
        cnt = jnp.minimum(pos + 1, k).astype(F32)
        pooled_ref[:, ch] = (wsum / cnt - u[:, ch]).astype(BF16)

    @pl.when(t == nt - 1)
    def _():
        nconv_ref[...] = xpad[CONV_PAD + valid - (CONV_K - 1):CONV_PAD + valid, :]
        npool_ref[...] = upad[POOL_PAD + valid - POOL_BUF:POOL_PAD + valid, :]
        for g in range(GROUPS):
            nssm_ref[g * GROUP_W:(g + 1) * GROUP_W, :] = hst[g].T

    @pl.when(t < nt - 1)
    def _():
        xpad[0:CONV_PAD, :] = xpad[t_len:t_len + CONV_PAD, :]
        upad[0:POOL_PAD, :] = upad[t_len:t_len + POOL_PAD, :]


def _ssd(xbc, dtr, z, u, convw, convb, dtb3, alog3, dskip_e, ng, e3, init, valid, pos0):
    b, s, _ = xbc.shape
    nt = s // CHUNK
    assert valid == CHUNK or nt == 1

    def seq(w):
        return pl.BlockSpec((None, CHUNK, w), lambda i, t: (i, t, 0))

    def per_b(r, w):
        return pl.BlockSpec((None, r, w), lambda i, t: (i, 0, 0))

    in_specs = [seq(CONV_DIM), seq(LANE), seq(D_INNER), seq(D_POOL),
                _const_spec((CONV_K, CONV_DIM)), _const_spec((1, CONV_DIM)), _const_spec((1, LANE)),
                _const_spec((1, LANE)), _const_spec((1, D_INNER)), _const_spec((1, D_INNER)),
                _const_spec((LANE, D_INNER))]
    args = [xbc, dtr, z, u, convw, convb, dtb3, alog3, dskip_e, ng, e3]
    if init is not None:
        in_specs += [per_b(CONV_PAD, CONV_DIM), per_b(D_INNER, STATE), per_b(POOL_PAD, D_POOL)]
        args += list(init)
    return pl.pallas_call(
        functools.partial(_ssd_kernel, nt=nt, valid=valid, pos0=pos0, has_init=init is not None),
        grid=(b, nt),
        in_specs=in_specs,
        out_specs=[seq(D_INNER), seq(D_POOL), per_b(CONV_K - 1, CONV_DIM), per_b(D_INNER, STATE),
                   per_b(POOL_BUF, D_POOL)],
        out_shape=[jax.ShapeDtypeStruct((b, s, D_INNER), BF16), jax.ShapeDtypeStruct((b, s, D_POOL), BF16),
                   jax.ShapeDtypeStruct((b, CONV_K - 1, CONV_DIM), F32),
                   jax.ShapeDtypeStruct((b, D_INNER, STATE), F32),
                   jax.ShapeDtypeStruct((b, POOL_BUF, D_POOL), F32)],
        scratch_shapes=[pltpu.VMEM((CONV_PAD + CHUNK, CONV_DIM), F32), pltpu.VMEM((POOL_PAD + CHUNK, D_POOL), F32),
                        pltpu.VMEM((GROUPS, STATE, GROUP_W), F32)],
        compiler_params=pltpu.CompilerParams(dimension_semantics=("parallel", "arbitrary"),
                                             vmem_limit_bytes=VMEM_LIMIT),
        name="ssd_pool",
    )(*args)


def _outproj_kernel(x_ref, yn_ref, pooled_ref, ga_ref, gb_ref, wps_ref, mix_ref, pscale_ref, wpp_ref, wout_ref,
                    post_ref, o_ref):
    y_a = _dot(yn_ref[...], wps_ref[...])
    pooled = pooled_ref[...]
    mixed = jnp.concatenate(
        [_dot(pooled[:, gi * POOL_GROUP_DIM:(gi + 1) * POOL_GROUP_DIM], mix_ref[gi])
         for gi in range(len(POOL_WINDOWS))], axis=-1)
    y_b = _dot((mixed * pscale_ref[...]).astype(BF16), wpp_ref[...])
    merged = jax.nn.sigmoid(ga_ref[...]) * y_a + jax.nn.sigmoid(gb_ref[...]) * y_b
    m = _dot(merged.astype(BF16), wout_ref[...])
    o_ref[...] = x_ref[...] + _rms(m, post_ref[...])


def _outproj(x, yn, pooled, ga, gb, wps, mix, pscale, wpp, wout, post_g, tm):
    n = x.shape[0]

    def row(w):
        return pl.BlockSpec((tm, w), lambda i: (i, 0))

    return pl.pallas_call(
        _outproj_kernel,
        grid=(n // tm,),
        in_specs=[row(D_MODEL), row(D_INNER), row(D_POOL), row(D_MODEL), row(D_MODEL),
                  _const_spec((D_INNER, D_MODEL)),
                  _const_spec((len(POOL_WINDOWS), POOL_GROUP_DIM, POOL_GROUP_DIM)), _const_spec((1, D_POOL)),
                  _const_spec((D_POOL, D_MODEL)), _const_spec((D_MODEL, D_MODEL)), _const_spec((1, D_MODEL))],
        out_specs=row(D_MODEL),
        out_shape=jax.ShapeDtypeStruct((n, D_MODEL), F32),
        compiler_params=pltpu.CompilerParams(dimension_semantics=("parallel",), vmem_limit_bytes=VMEM_LIMIT),
        name="outproj",
    )(x, yn, pooled, ga, gb, wps, mix, pscale, wpp, wout, post_g)


def _head_expansion():
    e = np.zeros((LANE, HEADS * HEAD_DIM), np.float32)
    for j in range(DT_REP):
        for h in range(HEADS):
            e[j * HEADS + h, h * HEAD_DIM:(h + 1) * HEAD_DIM] = 1.0
    return jnp.asarray(e, BF16)


def _rep_heads(v):
    return jnp.concatenate([v] * DT_REP + [jnp.zeros((LANE - DT_REP * HEADS,), F32)])[None, :]


def _prep(p):
    w_in = p["w_in"]
    cuts = np.cumsum([0, D_INNER, CONV_DIM, HEADS, D_POOL, D_MODEL, D_MODEL]).tolist()
    wz, wxbc, wdt, wu, wga, wgb = [w_in[:, a:b] for a, b in zip(cuts[:-1], cuts[1:])]
    wdt3 = jnp.concatenate([wdt] * DT_REP + [jnp.zeros((D_MODEL, LANE - DT_REP * HEADS), F32)], axis=1)
    q = {k: v for k, v in p.items()}
    for k in ("ffn1_w_gate", "ffn1_w_up", "ffn1_w_down", "ffn2_w_gate", "ffn2_w_up", "ffn2_w_down",
              "w_proj_ssd", "pool_mix", "w_proj_pool", "w_out"):
        q[k] = p[k].astype(BF16)
    for k in ("ffn1_pre_g", "ffn1_post_g", "mix_pre_g", "mix_post_g", "ffn2_pre_g", "ffn2_post_g", "conv_b",
              "ssd_norm_g", "pool_scale"):
        q[k] = p[k][None, :]
    q["in_w"] = [w.astype(BF16) for w in (wz, wxbc, wdt3, wu, wga, wgb)]
    q["dtb3"] = _rep_heads(p["dt_bias"])
    q["alog3"] = _rep_heads(p["a_log"])
    q["dskip_e"] = jnp.repeat(p["d_skip"], HEAD_DIM)[None, :]
    q["e3"] = _head_expansion()
    return q


def _layer(x, init, valid, pos0, q, tm):
    b, s, _ = x.shape
    xf = x.reshape(b * s, D_MODEL)
    xf = _ffn(xf, q["ffn1_pre_g"], q["ffn1_w_gate"], q["ffn1_w_up"], q["ffn1_w_down"], q["ffn1_post_g"], tm)
    z, xbc, dtr, u, ga, gb = _inproj(xf, q["mix_pre_g"], *q["in_w"], tm=min(tm, 256))
    yn, pooled, nconv, nssm, npool = _ssd(
        xbc.reshape(b, s, CONV_DIM), dtr.reshape(b, s, LANE), z.reshape(b, s, D_INNER), u.reshape(b, s, D_POOL),
        q["conv_w"], q["conv_b"], q["dtb3"], q["alog3"], q["dskip_e"], q["ssd_norm_g"], q["e3"], init, valid, pos0)
    xf = _outproj(xf, yn.reshape(b * s, D_INNER), pooled.reshape(b * s, D_POOL), ga, gb, q["w_proj_ssd"],
                  q["pool_mix"], q["pool_scale"], q["w_proj_pool"], q["w_out"], q["mix_post_g"], tm)
    xf = _ffn(xf, q["ffn2_pre_g"], q["ffn2_w_gate"], q["ffn2_w_up"], q["ffn2_w_down"], q["ffn2_post_g"], tm)
    return xf.reshape(b, s, D_MODEL), nconv, nssm.reshape(b, HEADS, HEAD_DIM, STATE), npool


def kernel(x_prompt, x_sample, cache_conv, state_ssm, cache_pool, ffn1_pre_g, ffn1_post_g, ffn1_w_gate, ffn1_w_up, ffn1_w_down, mix_pre_g, mix_post_g, w_in, conv_w, conv_b, dt_bias, a_log, d_skip, ssd_norm_g, w_proj_ssd, pool_mix, pool_scale, w_proj_pool, w_out, ffn2_pre_g, ffn2_post_g, ffn2_w_gate, ffn2_w_up, ffn2_w_down):
    names = ("ffn1_pre_g", "ffn1_post_g", "ffn1_w_gate", "ffn1_w_up", "ffn1_w_down", "mix_pre_g", "mix_post_g",
             "w_in", "conv_w", "conv_b", "dt_bias", "a_log", "d_skip", "ssd_norm_g", "w_proj_ssd", "pool_mix",
             "pool_scale", "w_proj_pool", "w_out", "ffn2_pre_g", "ffn2_post_g", "ffn2_w_gate", "ffn2_w_up",
             "ffn2_w_down")
    stacked = (ffn1_pre_g, ffn1_post_g, ffn1_w_gate, ffn1_w_up, ffn1_w_down, mix_pre_g, mix_post_g, w_in, conv_w,
               conv_b, dt_bias, a_log, d_skip, ssd_norm_g, w_proj_ssd, pool_mix, pool_scale, w_proj_pool, w_out,
               ffn2_pre_g, ffn2_post_g, ffn2_w_gate, ffn2_w_up, ffn2_w_down)
    depth = w_in.shape[0]
    dec_b, dec_s, _ = x_sample.shape
    assert dec_s <= CHUNK
    y_p = x_prompt
    y_s = jnp.pad(x_sample, ((0, 0), (0, CHUNK - dec_s), (0, 0)))
    outs = [[] for _ in range(6)]
    for i in range(depth):
        q = _prep({k: v[i] for k, v in zip(names, stacked)})
        y_p, c1, s1, q1 = _layer(y_p, None, CHUNK, 0, q, tm=512)
        init = (jnp.pad(cache_conv[i], ((0, 0), (CONV_PAD - (CONV_K - 1), 0), (0, 0))),
                state_ssm[i].reshape(dec_b, D_INNER, STATE),
                jnp.pad(cache_pool[i], ((0, 0), (POOL_PAD - POOL_BUF, 0), (0, 0))))
        y_s, c2, s2, q2 = _layer(y_s, init, dec_s, PAST_LEN, q, tm=dec_b * CHUNK)
        for lst, v in zip(outs, (c1, s1, q1, c2, s2, q2)):
            lst.append(v)
    return (y_p, y_s[:, :dec_s]) + tuple(jnp.stack(v) for v in outs)
```

```python
import functools

import numpy as np
import jax
import jax.numpy as jnp
from jax import lax
from jax.experimental import pallas as pl
from jax.experimental.pallas import tpu as pltpu

F32 = jnp.float32
BF16 = jnp.bfloat16

D_MODEL = 1024
D_FF = 2816
D_INNER = 2048
HEADS = 32
HEAD_DIM = 64
GROUPS = 8
HPG = HEADS // GROUPS
GROUP_W = HPG * HEAD_DIM
STATE = 128
CONV_K = 4
CONV_DIM = D_INNER + 2 * GROUPS * STATE
D_POOL = 1024
POOL_WINDOWS = (2, 4, 8, 16)
POOL_GROUP_DIM = D_POOL // len(POOL_WINDOWS)
POOL_BUF = max(POOL_WINDOWS) - 1
PAST_LEN = 4096
EPS = 1e-6
CHUNK = 64

LANE = 128
DT_REP = 3
CONV_PAD = 8
POOL_PAD = 16
IN_CUTS = (0, D_INNER, D_INNER + CONV_DIM, D_INNER + CONV_DIM + D_POOL, D_INNER + CONV_DIM + D_POOL + D_MODEL,
           D_INNER + CONV_DIM + D_POOL + 2 * D_MODEL)
IN_CHUNK = 1536
N_IN_CHUNKS = IN_CUTS[-1] // IN_CHUNK
MIX_ROWS = 128
VMEM_LIMIT = 56 * 1024 * 1024


def _const_spec(shape):
    nd = len(shape)
    return pl.BlockSpec(shape, lambda *_: (0,) * nd, pipeline_mode=pl.Buffered(1))


def _rms(x, g):
    ms = jnp.mean(x * x, axis=-1, keepdims=True)
    return x * lax.rsqrt(ms + EPS) * g


def _dot(a, b):
    return jnp.dot(a, b, preferred_element_type=F32)


def _silu(x):
    return x * jax.nn.sigmoid(x)


def _ffn_kernel(x_ref, pre_ref, wg_ref, wu_ref, wd_ref, post_ref, o_ref):
    x = x_ref[...]
    h = _rms(x, pre_ref[...]).astype(BF16)
    g = _dot(h, wg_ref[...])
    u = _dot(h, wu_ref[...])
    a = (_silu(g) * u).astype(BF16)
    f = _dot(a, wd_ref[...])
    o_ref[...] = x + 0.5 * _rms(f, post_ref[...])


def _ffn(x, pre_g, wg, wu, wd, post_g, tm):
    n = x.shape[0]
    row = pl.BlockSpec((tm, D_MODEL), lambda i: (i, 0))
    return pl.pallas_call(
        _ffn_kernel,
        grid=(n // tm,),
        in_specs=[row, _const_spec((1, D_MODEL)), _const_spec((D_MODEL, D_FF)), _const_spec((D_MODEL, D_FF)),
                  _const_spec((D_FF, D_MODEL)), _const_spec((1, D_MODEL))],
        out_specs=row,
        out_shape=jax.ShapeDtypeStruct((n, D_MODEL), F32),
        compiler_params=pltpu.CompilerParams(dimension_semantics=("parallel",), vmem_limit_bytes=VMEM_LIMIT),
        name="ffn",
    )(x, pre_g, wg, wu, wd, post_g)


def _expand_heads(v3, e3):
    hi = v3.astype(BF16).astype(F32)
    r1 = v3 - hi
    mid = r1.astype(BF16).astype(F32)
    lo = r1 - mid
    lane = lax.broadcasted_iota(jnp.int32, v3.shape, 1)
    packed = jnp.where(lane < HEADS, hi, jnp.where(lane < 2 * HEADS, mid, jnp.where(lane < 3 * HEADS, lo, 0.0)))
    return _dot(packed.astype(BF16), e3)


def _mixin_kernel(x_ref, pre_ref, *rest, t_len, nt, valid, pos0, has_init):
    w_refs, rest = rest[:N_IN_CHUNKS], rest[N_IN_CHUNKS:]
    wdt_ref, convw_ref, convb_ref, dtb_ref, alog_ref, dskip_ref, ng_ref, e3_ref, bdmask_ref = rest[:9]
    rest = rest[9:]
    if has_init:
        conv0_ref, ssm0_ref, pool0_ref = rest[:3]
        rest = rest[3:]
    yn_ref, pooled_ref, ga_ref, gb_ref, nconv_ref, nssm_ref, npool_ref, xpad, upad, xc, zs, hst = rest
    t = pl.program_id(1)
    n_xslab = CONV_DIM // LANE
    n_uslab = D_POOL // LANE

    def lanes(c):
        return slice(c * LANE, (c + 1) * LANE)

    @pl.when(t == 0)
    def _():
        for c in range(n_xslab):
            xpad[c, 0:CONV_PAD, :] = conv0_ref[:, lanes(c)] if has_init else jnp.zeros((CONV_PAD, LANE), F32)
        for c in range(n_uslab):
            upad[c, 0:POOL_PAD, :] = pool0_ref[:, lanes(c)] if has_init else jnp.zeros((POOL_PAD, LANE), F32)
        for g in range(GROUPS):
            hst[g] = ssm0_ref[g * GROUP_W:(g + 1) * GROUP_W, :].T if has_init else jnp.zeros((STATE, GROUP_W), F32)

    h = _rms(x_ref[...], pre_ref[...]).astype(BF16)
    for ci, w_ref in enumerate(w_refs):
        o = _dot(h, w_ref[...])
        for j in range(IN_CHUNK // LANE):
            col = ci * IN_CHUNK + j * LANE
            piece = o[:, lanes(j)]
            if col < IN_CUTS[1]:
                zs[:, col:col + LANE] = piece
            elif col < IN_CUTS[2]:
                xpad[(col - IN_CUTS[1]) // LANE, CONV_PAD:CONV_PAD + t_len, :] = piece
            elif col < IN_CUTS[3]:
                upad[(col - IN_CUTS[2]) // LANE, POOL_PAD:POOL_PAD + t_len, :] = piece
            elif col < IN_CUTS[4]:
                ga_ref[:, col - IN_CUTS[3]:col - IN_CUTS[3] + LANE] = piece
            else:
                gb_ref[:, col - IN_CUTS[4]:col - IN_CUTS[4] + LANE] = piece

    for c in range(n_xslab):
        conv = convb_ref[:, lanes(c)]
        for k in range(CONV_K):
            start = CONV_PAD - (CONV_K - 1) + k
            conv = conv + xpad[c, start:start + t_len, :] * convw_ref[k:k + 1, lanes(c)]
        xc[:, lanes(c)] = _silu(conv)
    b_off = D_INNER
    c_off = D_INNER + GROUPS * STATE

    v = _dot(h, wdt_ref[...]) + dtb_ref[...]
    dt3 = jnp.maximum(v, 0.0) + jnp.log1p(jnp.exp(-jnp.abs(v)))
    if valid < t_len:
        dt3 = jnp.where(lax.broadcasted_iota(jnp.int32, (t_len, LANE), 0) < valid, dt3, 0.0)
    a3 = -jnp.exp(alog_ref[...])
    r_i = lax.broadcasted_iota(jnp.int32, (t_len, t_len), 0)
    c_i = lax.broadcasted_iota(jnp.int32, (t_len, t_len), 1)
    tril = ((c_i <= r_i) & (c_i // CHUNK == r_i // CHUNK)).astype(F32)
    acs3 = jnp.dot(tril, dt3 * a3, precision=lax.Precision.HIGHEST, preferred_element_type=F32)
    e3 = e3_ref[...]
    dt_e = _expand_heads(dt3, e3)
    acs_e = _expand_heads(acs3, e3)

    li = lax.broadcasted_iota(jnp.int32, (CHUNK, GROUP_W), 0)
    s_of_lane = lax.broadcasted_iota(jnp.int32, (CHUNK, GROUP_W), 1) % HEAD_DIM
    diag = li == s_of_lane
    causal = li >= s_of_lane
    bdmask = bdmask_ref[...]

    for g in range(GROUPS):
        ch = slice(g * GROUP_W, (g + 1) * GROUP_W)
        h_state = hst[g]
        for c in range(t_len // CHUNK):
            rows = slice(c * CHUNK, (c + 1) * CHUNK)
            ae = acs_e[rows, ch]
            acs_row = jnp.sum(jnp.where(diag, ae, 0.0), axis=0, keepdims=True)
            decay = jnp.exp(jnp.where(causal, ae - acs_row, -jnp.inf))
            bg = xc[rows, b_off + g * STATE:b_off + (g + 1) * STATE].astype(BF16)
            cg = xc[rows, c_off + g * STATE:c_off + (g + 1) * STATE].astype(BF16)
            scores = lax.dot_general(cg, jnp.concatenate([bg] * HPG, axis=0), (((1,), (1,)), ((), ())),
                                     preferred_element_type=F32)
            xg = xc[rows, ch]
            xdt = xg * dt_e[rows, ch]
            xdt_bd = jnp.concatenate([xdt.astype(BF16)] * HPG, axis=0) * bdmask
            y_diag = _dot((scores * decay).astype(BF16), xdt_bd)
            y_off = _dot(cg, h_state.astype(BF16)) * jnp.exp(ae)
            last = ae[CHUNK - 1:CHUNK, :]
            xw = (xdt * jnp.exp(last - ae)).astype(BF16)
            h_state = jnp.exp(last) * h_state + lax.dot_general(bg, xw, (((0,), (0,)), ((), ())),
                                                                preferred_element_type=F32)
            y = (y_diag + y_off + dskip_ref[:, ch] * xg) * _silu(zs[rows, ch])
            ms = jnp.mean(y * y, axis=-1, keepdims=True)
            yn_ref[rows, ch] = (y * lax.rsqrt(ms + EPS) * ng_ref[:, ch]).astype(BF16)
        hst[g] = h_state

    pos = pos0 + t * t_len + lax.broadcasted_iota(jnp.int32, (t_len, 1), 0)
    for c in range(n_uslab):
        k = POOL_WINDOWS[c * LANE // POOL_GROUP_DIM]
        wsum = upad[c, POOL_PAD - (k - 1):POOL_PAD - (k - 1) + t_len, :]
        for j in range(k - 2, -1, -1):
            wsum = wsum + upad[c, POOL_PAD - j:POOL_PAD - j + t_len, :]
        cnt = jnp.minimum(pos + 1, k).astype(F32)
        pooled_ref[:, lanes(c)] = (wsum / cnt - upad[c, POOL_PAD:POOL_PAD + t_len, :]).astype(BF16)

    @pl.when(t == nt - 1)
    def _():
        for c in range(n_xslab):
            nconv_ref[:, lanes(c)] = xpad[c, CONV_PAD + valid - (CONV_K - 1):CONV_PAD + valid, :]
        for c in range(n_uslab):
            npool_ref[:, lanes(c)] = upad[c, POOL_PAD + valid - POOL_BUF:POOL_PAD + valid, :]
        for g in range(GROUPS):
            nssm_ref[g * GROUP_W:(g + 1) * GROUP_W, :] = hst[g].T

    @pl.when(t < nt - 1)
    def _():
        for c in range(n_xslab):
            xpad[c, 0:CONV_PAD, :] = xpad[c, t_len:t_len + CONV_PAD, :]
        for c in range(n_uslab):
            upad[c, 0:POOL_PAD, :] = upad[c, t_len:t_len + POOL_PAD, :]


def _mixin(x, q, init, valid, pos0):
    b, s, _ = x.shape
    t_len = min(s, MIX_ROWS)
    nt = s // t_len
    valid = t_len if valid is None else valid
    assert s % t_len == 0 and t_len % CHUNK == 0 and (valid == t_len or nt == 1)

    def seq(w):
        return pl.BlockSpec((None, t_len, w), lambda i, t: (i, t, 0))

    def per_b(r, w):
        return pl.BlockSpec((None, r, w), lambda i, t: (i, 0, 0))

    in_specs = ([seq(D_MODEL), _const_spec((1, D_MODEL))] + [_const_spec((D_MODEL, IN_CHUNK))] * N_IN_CHUNKS +
                [_const_spec((D_MODEL, LANE)), _const_spec((CONV_K, CONV_DIM)), _const_spec((1, CONV_DIM)),
                 _const_spec((1, LANE)), _const_spec((1, LANE)), _const_spec((1, D_INNER)),
                 _const_spec((1, D_INNER)), _const_spec((LANE, D_INNER)), _const_spec((HPG * CHUNK, GROUP_W))])
    args = [x, q["mix_pre_g"]] + q["in_w"] + [q["in_wdt"], q["conv_w"], q["conv_b"], q["dtb3"], q["alog3"],
                                             q["dskip_e"], q["ssd_norm_g"], q["e3"], q["bdmask"]]
    if init is not None:
        in_specs += [per_b(CONV_PAD, CONV_DIM), per_b(D_INNER, STATE), per_b(POOL_PAD, D_POOL)]
        args += list(init)
    return pl.pallas_call(
        functools.partial(_mixin_kernel, t_len=t_len, nt=nt, valid=valid, pos0=pos0, has_init=init is not None),
        grid=(b, nt),
        in_specs=in_specs,
        out_specs=[seq(D_INNER), seq(D_POOL), seq(D_MODEL), seq(D_MODEL), per_b(CONV_K - 1, CONV_DIM),
                   per_b(D_INNER, STATE), per_b(POOL_BUF, D_POOL)],
        out_shape=[jax.ShapeDtypeStruct((b, s, D_INNER), BF16), jax.ShapeDtypeStruct((b, s, D_POOL), BF16),
                   jax.ShapeDtypeStruct((b, s, D_MODEL), F32), jax.ShapeDtypeStruct((b, s, D_MODEL), F32),
                   jax.ShapeDtypeStruct((b, CONV_K - 1, CONV_DIM), F32),
                   jax.ShapeDtypeStruct((b, D_INNER, STATE), F32),
                   jax.ShapeDtypeStruct((b, POOL_BUF, D_POOL), F32)],
        scratch_shapes=[pltpu.VMEM((CONV_DIM // LANE, CONV_PAD + t_len, LANE), F32),
                        pltpu.VMEM((D_POOL // LANE, POOL_PAD + t_len, LANE), F32),
                        pltpu.VMEM((t_len, CONV_DIM), F32), pltpu.VMEM((t_len, D_INNER), F32),
                        pltpu.VMEM((GROUPS, STATE, GROUP_W), F32)],
        compiler_params=pltpu.CompilerParams(dimension_semantics=("parallel", "arbitrary"),
                                             vmem_limit_bytes=VMEM_LIMIT),
        name="mixin",
    )(*args)


def _outproj_kernel(x_ref, yn_ref, pooled_ref, ga_ref, gb_ref, wps_ref, mix_ref, pscale_ref, wpp_ref, wout_ref,
                    post_ref, o_ref):
    y_a = _dot(yn_ref[...], wps_ref[...])
    pooled = pooled_ref[...]
    mixed = jnp.concatenate(
        [_dot(pooled[:, gi * POOL_GROUP_DIM:(gi + 1) * POOL_GROUP_DIM], mix_ref[gi])
         for gi in range(len(POOL_WINDOWS))], axis=-1)
    y_b = _dot((mixed * pscale_ref[...]).astype(BF16), wpp_ref[...])
    merged = jax.nn.sigmoid(ga_ref[...]) * y_a + jax.nn.sigmoid(gb_ref[...]) * y_b
    m = _dot(merged.astype(BF16), wout_ref[...])
    o_ref[...] = x_ref[...] + _rms(m, post_ref[...])


def _outproj(x, yn, pooled, ga, gb, wps, mix, pscale, wpp, wout, post_g, tm):
    n = x.shape[0]

    def row(w):
        return pl.BlockSpec((tm, w), lambda i: (i, 0))

    return pl.pallas_call(
        _outproj_kernel,
        grid=(n // tm,),
        in_specs=[row(D_MODEL), row(D_INNER), row(D_POOL), row(D_MODEL), row(D_MODEL),
                  _const_spec((D_INNER, D_MODEL)),
                  _const_spec((len(POOL_WINDOWS), POOL_GROUP_DIM, POOL_GROUP_DIM)), _const_spec((1, D_POOL)),
                  _const_spec((D_POOL, D_MODEL)), _const_spec((D_MODEL, D_MODEL)), _const_spec((1, D_MODEL))],
        out_specs=row(D_MODEL),
        out_shape=jax.ShapeDtypeStruct((n, D_MODEL), F32),
        compiler_params=pltpu.CompilerParams(dimension_semantics=("parallel",), vmem_limit_bytes=VMEM_LIMIT),
        name="outproj",
    )(x, yn, pooled, ga, gb, wps, mix, pscale, wpp, wout, post_g)


def _head_expansion():
    e = np.zeros((LANE, HEADS * HEAD_DIM), np.float32)
    for j in range(DT_REP):
        for h in range(HEADS):
            e[j * HEADS + h, h * HEAD_DIM:(h + 1) * HEAD_DIM] = 1.0
    return jnp.asarray(e, BF16)


def _head_blockdiag():
    r = np.arange(HPG * CHUNK)[:, None] // CHUNK
    c = np.arange(GROUP_W)[None, :] // HEAD_DIM
    return jnp.asarray((r == c).astype(np.float32), BF16)


def _rep_heads(v):
    return jnp.concatenate([v] * DT_REP + [jnp.zeros((LANE - DT_REP * HEADS,), F32)])[None, :]


def _prep(p):
    w_in = p["w_in"]
    cuts = np.cumsum([0, D_INNER, CONV_DIM, HEADS, D_POOL, D_MODEL, D_MODEL]).tolist()
    wz, wxbc, wdt, wu, wga, wgb = [w_in[:, a:b] for a, b in zip(cuts[:-1], cuts[1:])]
    wdt3 = jnp.concatenate([wdt] * DT_REP + [jnp.zeros((D_MODEL, LANE - DT_REP * HEADS), F32)], axis=1)
    w_main = jnp.concatenate([wz, wxbc, wu, wga, wgb], axis=1).astype(BF16)
    q = {k: v for k, v in p.items()}
    for k in ("ffn1_w_gate", "ffn1_w_up", "ffn1_w_down", "ffn2_w_gate", "ffn2_w_up", "ffn2_w_down",
              "w_proj_ssd", "pool_mix", "w_proj_pool", "w_out"):
        q[k] = p[k].astype(BF16)
    for k in ("ffn1_pre_g", "ffn1_post_g", "mix_pre_g", "mix_post_g", "ffn2_pre_g", "ffn2_post_g", "conv_b",
              "ssd_norm_g", "pool_scale"):
        q[k] = p[k][None, :]
    q["in_w"] = [w_main[:, i * IN_CHUNK:(i + 1) * IN_CHUNK] for i in range(N_IN_CHUNKS)]
    q["in_wdt"] = wdt3.astype(BF16)
    q["dtb3"] = _rep_heads(p["dt_bias"])
    q["alog3"] = _rep_heads(p["a_log"])
    q["dskip_e"] = jnp.repeat(p["d_skip"], HEAD_DIM)[None, :]
    q["e3"] = _head_expansion()
    q["bdmask"] = _head_blockdiag()
    return q


def _layer(x, init, valid, pos0, q, tm):
    b, s, _ = x.shape
    n = b * s
    xf = x.reshape(n, D_MODEL)
    xf = _ffn(xf, q["ffn1_pre_g"], q["ffn1_w_gate"], q["ffn1_w_up"], q["ffn1_w_down"], q["ffn1_post_g"], tm)
    yn, pooled, ga, gb, nconv, nssm, npool = _mixin(xf.reshape(b, s, D_MODEL), q, init, valid, pos0)
    xf = _outproj(xf, yn.reshape(n, D_INNER), pooled.reshape(n, D_POOL), ga.reshape(n, D_MODEL),
                  gb.reshape(n, D_MODEL), q["w_proj_ssd"], q["pool_mix"], q["pool_scale"], q["w_proj_pool"],
                  q["w_out"], q["mix_post_g"], tm)
    xf = _ffn(xf, q["ffn2_pre_g"], q["ffn2_w_gate"], q["ffn2_w_up"], q["ffn2_w_down"], q["ffn2_post_g"], tm)
    return xf.reshape(b, s, D_MODEL), nconv, nssm.reshape(b, HEADS, HEAD_DIM, STATE), npool


def kernel(x_prompt, x_sample, cache_conv, state_ssm, cache_pool, ffn1_pre_g, ffn1_post_g, ffn1_w_gate, ffn1_w_up, ffn1_w_down, mix_pre_g, mix_post_g, w_in, conv_w, conv_b, dt_bias, a_log, d_skip, ssd_norm_g, w_proj_ssd, pool_mix, pool_scale, w_proj_pool, w_out, ffn2_pre_g, ffn2_post_g, ffn2_w_gate, ffn2_w_up, ffn2_w_down):
    names = ("ffn1_pre_g", "ffn1_post_g", "ffn1_w_gate", "ffn1_w_up", "ffn1_w_down", "mix_pre_g", "mix_post_g",
             "w_in", "conv_w", "conv_b", "dt_bias", "a_log", "d_skip", "ssd_norm_g", "w_proj_ssd", "pool_mix",
             "pool_scale", "w_proj_pool", "w_out", "ffn2_pre_g", "ffn2_post_g", "ffn2_w_gate", "ffn2_w_up",
             "ffn2_w_down")
    stacked = (ffn1_pre_g, ffn1_post_g, ffn1_w_gate, ffn1_w_up, ffn1_w_down, mix_pre_g, mix_post_g, w_in, conv_w,
               conv_b, dt_bias, a_log, d_skip, ssd_norm_g, w_proj_ssd, pool_mix, pool_scale, w_proj_pool, w_out,
               ffn2_pre_g, ffn2_post_g, ffn2_w_gate, ffn2_w_up, ffn2_w_down)
    depth = w_in.shape[0]
    dec_b, dec_s, _ = x_sample.shape
    assert dec_s <= CHUNK
    y_p = x_prompt
    y_s = jnp.pad(x_sample, ((0, 0), (0, CHUNK - dec_s), (0, 0)))
    outs = [[] for _ in range(6)]
    for i in range(depth):
        q = _prep({k: v[i] for k, v in zip(names, stacked)})
        y_p, c1, s1, q1 = _layer(y_p, None, None, 0, q, tm=512)
        init = (jnp.pad(cache_conv[i], ((0, 0), (CONV_PAD - (CONV_K - 1), 0), (0, 0))),
                state_ssm[i].reshape(dec_b, D_INNER, STATE),
                jnp.pad(cache_pool[i], ((0, 0), (POOL_PAD - POOL_BUF, 0), (0, 0))))
        y_s, c2, s2, q2 = _layer(y_s, init, dec_s, PAST_LEN, q, tm=dec_b * CHUNK)
        for lst, v in zip(outs, (c1, s1, q1, c2, s2, q2)):
            lst.append(v)
    return (y_p, y_s[:, :dec_s]) + tuple(jnp.stack(v) for v in outs)
```

```python
import functools

import numpy as np
import jax
import jax.numpy as jnp
from jax import lax
from jax.experimental import pallas as pl
from jax.experimental.pallas import tpu as pltpu

F32 = jnp.float32
BF16 = jnp.bfloat16

D_MODEL = 1024
D_FF = 2816
D_INNER = 2048
HEADS = 32
HEAD_DIM = 64
GROUPS = 8
HPG = HEADS // GROUPS
GROUP_W = HPG * HEAD_DIM
STATE = 128
CONV_K = 4
CONV_DIM = D_INNER + 2 * GROUPS * STATE
D_POOL = 1024
POOL_WINDOWS = (2, 4, 8, 16)
POOL_GROUP_DIM = D_POOL // len(POOL_WINDOWS)
POOL_BUF = max(POOL_WINDOWS) - 1
PAST_LEN = 4096
EPS = 1e-6
CHUNK = 64

LANE = 128
DT_REP = 3
CONV_PAD = 8
POOL_PAD = 16
IN_CUTS = (0, D_INNER, D_INNER + CONV_DIM, D_INNER + CONV_DIM + D_POOL, D_INNER + CONV_DIM + D_POOL + D_MODEL,
           D_INNER + CONV_DIM + D_POOL + 2 * D_MODEL)
IN_CHUNK = 1536
N_IN_CHUNKS = IN_CUTS[-1] // IN_CHUNK
GROUP_COLS = 2 * GROUP_W + 2 * STATE
GROUPS_PER_CHUNK = IN_CHUNK // GROUP_COLS
MIX_ROWS = 256
VMEM_LIMIT = 56 * 1024 * 1024


def _const_spec(shape):
    nd = len(shape)
    return pl.BlockSpec(shape, lambda *_: (0,) * nd, pipeline_mode=pl.Buffered(1))


def _rms(x, g):
    ms = jnp.mean(x * x, axis=-1, keepdims=True)
    return x * lax.rsqrt(ms + EPS) * g


def _dot(a, b):
    return jnp.dot(a, b, preferred_element_type=F32)


def _silu(x):
    return x * jax.nn.sigmoid(x)


def _softplus(v):
    e = jnp.exp(-jnp.abs(v))
    u = 1.0 + e
    tiny = u == 1.0
    return jnp.maximum(v, 0.0) + jnp.where(tiny, e, jnp.log(u) * e / jnp.where(tiny, 1.0, u - 1.0))


def _ffn_kernel(x_ref, pre_ref, wg_ref, wu_ref, wd_ref, post_ref, o_ref):
    x = x_ref[...]
    h = _rms(x, pre_ref[...]).astype(BF16)
    g = _dot(h, wg_ref[...])
    u = _dot(h, wu_ref[...])
    a = (_silu(g) * u).astype(BF16)
    f = _dot(a, wd_ref[...])
    o_ref[...] = x + 0.5 * _rms(f, post_ref[...])


def _ffn(x, pre_g, wg, wu, wd, post_g, tm):
    n = x.shape[0]
    row = pl.BlockSpec((tm, D_MODEL), lambda i: (i, 0))
    return pl.pallas_call(
        _ffn_kernel,
        grid=(n // tm,),
        in_specs=[row, _const_spec((1, D_MODEL)), _const_spec((D_MODEL, D_FF)), _const_spec((D_MODEL, D_FF)),
                  _const_spec((D_FF, D_MODEL)), _const_spec((1, D_MODEL))],
        out_specs=row,
        out_shape=jax.ShapeDtypeStruct((n, D_MODEL), F32),
        compiler_params=pltpu.CompilerParams(dimension_semantics=("parallel",), vmem_limit_bytes=VMEM_LIMIT),
        name="ffn",
    )(x, pre_g, wg, wu, wd, post_g)


def _expand_heads(v3, e3):
    hi = v3.astype(BF16).astype(F32)
    r1 = v3 - hi
    mid = r1.astype(BF16).astype(F32)
    lo = r1 - mid
    lane = lax.broadcasted_iota(jnp.int32, v3.shape, 1)
    packed = jnp.where(lane < HEADS, hi, jnp.where(lane < 2 * HEADS, mid, jnp.where(lane < 3 * HEADS, lo, 0.0)))
    return _dot(packed.astype(BF16), e3)


def _in_col_order():
    cols = []
    for g in range(GROUPS):
        cols.append(np.arange(g * GROUP_W, (g + 1) * GROUP_W))
        cols.append(IN_CUTS[1] + np.arange(g * GROUP_W, (g + 1) * GROUP_W))
        cols.append(IN_CUTS[1] + D_INNER + np.arange(g * STATE, (g + 1) * STATE))
        cols.append(IN_CUTS[1] + D_INNER + GROUPS * STATE + np.arange(g * STATE, (g + 1) * STATE))
    cols.append(np.arange(IN_CUTS[2], IN_CUTS[5]))
    return np.concatenate(cols)


def _in_col_dest(col):
    src = int(_in_col_order()[col])
    for kind, lo, hi in zip(("z", "xbc", "u", "ga", "gb"), IN_CUTS[:-1], IN_CUTS[1:]):
        if lo <= src < hi:
            return kind, src - lo
    raise ValueError(col)


def _mixin_kernel(x_ref, pre_ref, *rest, t_len, nt, valid, pos0, has_init):
    w_refs, rest = rest[:N_IN_CHUNKS], rest[N_IN_CHUNKS:]
    wdt_ref, convw_ref, convb_ref, dtb_ref, alog_ref, dskip_ref, ng_ref, e3_ref, bdmask_ref = rest[:9]
    rest = rest[9:]
    if has_init:
        conv0_ref, ssm0_ref, pool0_ref = rest[:3]
        rest = rest[3:]
    yn_ref, pooled_ref, ga_ref, gb_ref, nconv_ref, nssm_ref, npool_ref, xpad, upad, xc, zs, hst = rest
    t = pl.program_id(1)
    n_xslab = CONV_DIM // LANE
    n_uslab = D_POOL // LANE

    def lanes(c):
        return slice(c * LANE, (c + 1) * LANE)

    @pl.when(t == 0)
    def _():
        for c in range(n_xslab):
            xpad[c, 0:CONV_PAD, :] = conv0_ref[:, lanes(c)] if has_init else jnp.zeros((CONV_PAD, LANE), F32)
        for c in range(n_uslab):
            upad[c, 0:POOL_PAD, :] = pool0_ref[:, lanes(c)] if has_init else jnp.zeros((POOL_PAD, LANE), F32)
        for g in range(GROUPS):
            hst[g] = ssm0_ref[g * GROUP_W:(g + 1) * GROUP_W, :].T if has_init else jnp.zeros((STATE, GROUP_W), F32)

    h = _rms(x_ref[...], pre_ref[...]).astype(BF16)

    v = _dot(h, wdt_ref[...]) + dtb_ref[...]
    dt3 = _softplus(v)
    if valid < t_len:
        dt3 = jnp.where(lax.broadcasted_iota(jnp.int32, (t_len, LANE), 0) < valid, dt3, 0.0)
    a3 = -jnp.exp(alog_ref[...])
    r_i = lax.broadcasted_iota(jnp.int32, (t_len, t_len), 0)
    c_i = lax.broadcasted_iota(jnp.int32, (t_len, t_len), 1)
    tril = ((c_i <= r_i) & (c_i // CHUNK == r_i // CHUNK)).astype(F32)
    acs3 = jnp.dot(tril, dt3 * a3, precision=lax.Precision.HIGHEST, preferred_element_type=F32)
    e3 = e3_ref[...]
    dt_e = _expand_heads(dt3, e3)
    acs_e = _expand_heads(acs3, e3)

    li = lax.broadcasted_iota(jnp.int32, (CHUNK, GROUP_W), 0)
    s_of_lane = lax.broadcasted_iota(jnp.int32, (CHUNK, GROUP_W), 1) % HEAD_DIM
    diag = li == s_of_lane
    causal = li >= s_of_lane
    bdmask = bdmask_ref[...]
    b_off = D_INNER
    c_off = D_INNER + GROUPS * STATE

    def project(ci):
        o = _dot(h, w_refs[ci][...])
        for j in range(IN_CHUNK // LANE):
            kind, off = _in_col_dest(ci * IN_CHUNK + j * LANE)
            piece = o[:, lanes(j)]
            if kind == "z":
                zs[:, off:off + LANE] = piece
            elif kind == "xbc":
                xpad[off // LANE, CONV_PAD:CONV_PAD + t_len, :] = piece
            elif kind == "u":
                upad[off // LANE, POOL_PAD:POOL_PAD + t_len, :] = piece
            elif kind == "ga":
                ga_ref[:, off:off + LANE] = piece
            else:
                gb_ref[:, off:off + LANE] = piece

    def conv_slab(c):
        conv = convb_ref[:, lanes(c)]
        for k in range(CONV_K):
            start = CONV_PAD - (CONV_K - 1) + k
            conv = conv + xpad[c, start:start + t_len, :] * convw_ref[k:k + 1, lanes(c)]
        xc[:, lanes(c)] = _silu(conv)

    def scan_group(g):
        ch = slice(g * GROUP_W, (g + 1) * GROUP_W)
        h_state = hst[g]
        for c in range(t_len // CHUNK):
            rows = slice(c * CHUNK, (c + 1) * CHUNK)
            ae = acs_e[rows, ch]
            acs_row = jnp.sum(jnp.where(diag, ae, 0.0), axis=0, keepdims=True)
            decay = jnp.exp(jnp.where(causal, ae - acs_row, -jnp.inf))
            bg = xc[rows, b_off + g * STATE:b_off + (g + 1) * STATE].astype(BF16)
            cg = xc[rows, c_off + g * STATE:c_off + (g + 1) * STATE].astype(BF16)
            scores = lax.dot_general(cg, jnp.concatenate([bg] * HPG, axis=0), (((1,), (1,)), ((), ())),
                                     preferred_element_type=F32)
            xg = xc[rows, ch]
            xdt = xg * dt_e[rows, ch]
            xdt_bd = jnp.concatenate([xdt.astype(BF16)] * HPG, axis=0) * bdmask
            y_diag = _dot((scores * decay).astype(BF16), xdt_bd)
            y_off = _dot(cg, h_state.astype(BF16)) * jnp.exp(ae)
            last = ae[CHUNK - 1:CHUNK, :]
            xw = (xdt * jnp.exp(last - ae)).astype(BF16)
            h_state = jnp.exp(last) * h_state + lax.dot_general(bg, xw, (((0,), (0,)), ((), ())),
                                                                preferred_element_type=F32)
            y = (y_diag + y_off + dskip_ref[:, ch] * xg) * _silu(zs[rows, ch])
            ms = jnp.mean(y * y, axis=-1, keepdims=True)
            yn_ref[rows, ch] = (y * lax.rsqrt(ms + EPS) * ng_ref[:, ch]).astype(BF16)
        hst[g] = h_state

    for ci in range(N_IN_CHUNKS):
        project(ci)
        for g in range(ci * GROUPS_PER_CHUNK, min((ci + 1) * GROUPS_PER_CHUNK, GROUPS)):
            for c in (2 * g, 2 * g + 1, b_off // LANE + g, c_off // LANE + g):
                conv_slab(c)
            scan_group(g)

    pos = pos0 + t * t_len + lax.broadcasted_iota(jnp.int32, (t_len, 1), 0)
    for c in range(n_uslab):
        k = POOL_WINDOWS[c * LANE // POOL_GROUP_DIM]
        wsum = upad[c, POOL_PAD - (k - 1):POOL_PAD - (k - 1) + t_len, :]
        for j in range(k - 2, -1, -1):
            wsum = wsum + upad[c, POOL_PAD - j:POOL_PAD - j + t_len, :]
        cnt = jnp.minimum(pos + 1, k).astype(F32)
        pooled_ref[:, lanes(c)] = (wsum / cnt - upad[c, POOL_PAD:POOL_PAD + t_len, :]).astype(BF16)

    @pl.when(t == nt - 1)
    def _():
        for c in range(n_xslab):
            nconv_ref[:, lanes(c)] = xpad[c, CONV_PAD + valid - (CONV_K - 1):CONV_PAD + valid, :]
        for c in range(n_uslab):
            npool_ref[:, lanes(c)] = upad[c, POOL_PAD + valid - POOL_BUF:POOL_PAD + valid, :]
        for g in range(GROUPS):
            nssm_ref[g * GROUP_W:(g + 1) * GROUP_W, :] = hst[g].T

    @pl.when(t < nt - 1)
    def _():
        for c in range(n_xslab):
            xpad[c, 0:CONV_PAD, :] = xpad[c, t_len:t_len + CONV_PAD, :]
        for c in range(n_uslab):
            upad[c, 0:POOL_PAD, :] = upad[c, t_len:t_len + POOL_PAD, :]


def _mixin(x, q, init, valid, pos0):
    b, s, _ = x.shape
    t_len = min(s, MIX_ROWS)
    nt = s // t_len
    valid = t_len if valid is None else valid
    assert s % t_len == 0 and t_len % CHUNK == 0 and (valid == t_len or nt == 1)

    def seq(w):
        return pl.BlockSpec((None, t_len, w), lambda i, t: (i, t, 0))

    def per_b(r, w):
        return pl.BlockSpec((None, r, w), lambda i, t: (i, 0, 0))

    in_specs = ([seq(D_MODEL), _const_spec((1, D_MODEL))] + [_const_spec((D_MODEL, IN_CHUNK))] * N_IN_CHUNKS +
                [_const_spec((D_MODEL, LANE)), _const_spec((CONV_K, CONV_DIM)), _const_spec((1, CONV_DIM)),
                 _const_spec((1, LANE)), _const_spec((1, LANE)), _const_spec((1, D_INNER)),
                 _const_spec((1, D_INNER)), _const_spec((LANE, D_INNER)), _const_spec((HPG * CHUNK, GROUP_W))])
    args = [x, q["mix_pre_g"]] + q["in_w"] + [q["in_wdt"], q["conv_w"], q["conv_b"], q["dtb3"], q["alog3"],
                                             q["dskip_e"], q["ssd_norm_g"], q["e3"], q["bdmask"]]
    if init is not None:
        in_specs += [per_b(CONV_PAD, CONV_DIM), per_b(D_INNER, STATE), per_b(POOL_PAD, D_POOL)]
        args += list(init)
    return pl.pallas_call(
        functools.partial(_mixin_kernel, t_len=t_len, nt=nt, valid=valid, pos0=pos0, has_init=init is not None),
        grid=(b, nt),
        in_specs=in_specs,
        out_specs=[seq(D_INNER), seq(D_POOL), seq(D_MODEL), seq(D_MODEL), per_b(CONV_K - 1, CONV_DIM),
                   per_b(D_INNER, STATE), per_b(POOL_BUF, D_POOL)],
        out_shape=[jax.ShapeDtypeStruct((b, s, D_INNER), BF16), jax.ShapeDtypeStruct((b, s, D_POOL), BF16),
                   jax.ShapeDtypeStruct((b, s, D_MODEL), F32), jax.ShapeDtypeStruct((b, s, D_MODEL), F32),
                   jax.ShapeDtypeStruct((b, CONV_K - 1, CONV_DIM), F32),
                   jax.ShapeDtypeStruct((b, D_INNER, STATE), F32),
                   jax.ShapeDtypeStruct((b, POOL_BUF, D_POOL), F32)],
        scratch_shapes=[pltpu.VMEM((CONV_DIM // LANE, CONV_PAD + t_len, LANE), F32),
                        pltpu.VMEM((D_POOL // LANE, POOL_PAD + t_len, LANE), F32),
                        pltpu.VMEM((t_len, CONV_DIM), F32), pltpu.VMEM((t_len, D_INNER), F32),
                        pltpu.VMEM((GROUPS, STATE, GROUP_W), F32)],
        compiler_params=pltpu.CompilerParams(dimension_semantics=("parallel", "arbitrary"),
                                             vmem_limit_bytes=VMEM_LIMIT),
        name="mixin",
    )(*args)


def _outproj_kernel(x_ref, yn_ref, pooled_ref, ga_ref, gb_ref, wps_ref, mix_ref, pscale_ref, wpp_ref, wout_ref,
                    post_ref, o_ref):
    y_a = _dot(yn_ref[...], wps_ref[...])
    pooled = pooled_ref[...]
    mixed = jnp.concatenate(
        [_dot(pooled[:, gi * POOL_GROUP_DIM:(gi + 1) * POOL_GROUP_DIM], mix_ref[gi])
         for gi in range(len(POOL_WINDOWS))], axis=-1)
    y_b = _dot((mixed * pscale_ref[...]).astype(BF16), wpp_ref[...])
    merged = jax.nn.sigmoid(ga_ref[...]) * y_a + jax.nn.sigmoid(gb_ref[...]) * y_b
    m = _dot(merged.astype(BF16), wout_ref[...])
    o_ref[...] = x_ref[...] + _rms(m, post_ref[...])


def _outproj(x, yn, pooled, ga, gb, wps, mix, pscale, wpp, wout, post_g, tm):
    n = x.shape[0]

    def row(w):
        return pl.BlockSpec((tm, w), lambda i: (i, 0))

    return pl.pallas_call(
        _outproj_kernel,
        grid=(n // tm,),
        in_specs=[row(D_MODEL), row(D_INNER), row(D_POOL), row(D_MODEL), row(D_MODEL),
                  _const_spec((D_INNER, D_MODEL)),
                  _const_spec((len(POOL_WINDOWS), POOL_GROUP_DIM, POOL_GROUP_DIM)), _const_spec((1, D_POOL)),
                  _const_spec((D_POOL, D_MODEL)), _const_spec((D_MODEL, D_MODEL)), _const_spec((1, D_MODEL))],
        out_specs=row(D_MODEL),
        out_shape=jax.ShapeDtypeStruct((n, D_MODEL), F32),
        compiler_params=pltpu.CompilerParams(dimension_semantics=("parallel",), vmem_limit_bytes=VMEM_LIMIT),
        name="outproj",
    )(x, yn, pooled, ga, gb, wps, mix, pscale, wpp, wout, post_g)


def _head_expansion():
    e = np.zeros((LANE, HEADS * HEAD_DIM), np.float32)
    for j in range(DT_REP):
        for h in range(HEADS):
            e[j * HEADS + h, h * HEAD_DIM:(h + 1) * HEAD_DIM] = 1.0
    return jnp.asarray(e, BF16)


def _head_blockdiag():
    r = np.arange(HPG * CHUNK)[:, None] // CHUNK
    c = np.arange(GROUP_W)[None, :] // HEAD_DIM
    return jnp.asarray((r == c).astype(np.float32), BF16)


def _rep_heads(v):
    return jnp.concatenate([v] * DT_REP + [jnp.zeros((LANE - DT_REP * HEADS,), F32)])[None, :]


def _prep(p):
    w_in = p["w_in"]
    cuts = np.cumsum([0, D_INNER, CONV_DIM, HEADS, D_POOL, D_MODEL, D_MODEL]).tolist()
    wz, wxbc, wdt, wu, wga, wgb = [w_in[:, a:b] for a, b in zip(cuts[:-1], cuts[1:])]
    wdt3 = jnp.concatenate([wdt] * DT_REP + [jnp.zeros((D_MODEL, LANE - DT_REP * HEADS), F32)], axis=1)
    w_main = jnp.concatenate([wz, wxbc, wu, wga, wgb], axis=1).astype(BF16)[:, _in_col_order()]
    q = {k: v for k, v in p.items()}
    for k in ("ffn1_w_gate", "ffn1_w_up", "ffn1_w_down", "ffn2_w_gate", "ffn2_w_up", "ffn2_w_down",
              "w_proj_ssd", "pool_mix", "w_proj_pool", "w_out"):
        q[k] = p[k].astype(BF16)
    for k in ("ffn1_pre_g", "ffn1_post_g", "mix_pre_g", "mix_post_g", "ffn2_pre_g", "ffn2_post_g", "conv_b",
              "ssd_norm_g", "pool_scale"):
        q[k] = p[k][None, :]
    q["in_w"] = [w_main[:, i * IN_CHUNK:(i + 1) * IN_CHUNK] for i in range(N_IN_CHUNKS)]
    q["in_wdt"] = wdt3.astype(BF16)
    q["dtb3"] = _rep_heads(p["dt_bias"])
    q["alog3"] = _rep_heads(p["a_log"])
    q["dskip_e"] = jnp.repeat(p["d_skip"], HEAD_DIM)[None, :]
    q["e3"] = _head_expansion()
    q["bdmask"] = _head_blockdiag()
    return q


def _layer(x, init, valid, pos0, q, tm):
    b, s, _ = x.shape
    n = b * s
    xf = x.reshape(n, D_MODEL)
    xf = _ffn(xf, q["ffn1_pre_g"], q["ffn1_w_gate"], q["ffn1_w_up"], q["ffn1_w_down"], q["ffn1_post_g"], tm)
    yn, pooled, ga, gb, nconv, nssm, npool = _mixin(xf.reshape(b, s, D_MODEL), q, init, valid, pos0)
    xf = _outproj(xf, yn.reshape(n, D_INNER), pooled.reshape(n, D_POOL), ga.reshape(n, D_MODEL),
                  gb.reshape(n, D_MODEL), q["w_proj_ssd"], q["pool_mix"], q["pool_scale"], q["w_proj_pool"],
                  q["w_out"], q["mix_post_g"], tm)
    xf = _ffn(xf, q["ffn2_pre_g"], q["ffn2_w_gate"], q["ffn2_w_up"], q["ffn2_w_down"], q["ffn2_post_g"], tm)
    return xf.reshape(b, s, D_MODEL), nconv, nssm.reshape(b, HEADS, HEAD_DIM, STATE), npool


def kernel(x_prompt, x_sample, cache_conv, state_ssm, cache_pool, ffn1_pre_g, ffn1_post_g, ffn1_w_gate, ffn1_w_up, ffn1_w_down, mix_pre_g, mix_post_g, w_in, conv_w, conv_b, dt_bias, a_log, d_skip, ssd_norm_g, w_proj_ssd, pool_mix, pool_scale, w_proj_pool, w_out, ffn2_pre_g, ffn2_post_g, ffn2_w_gate, ffn2_w_up, ffn2_w_down):
    names = ("ffn1_pre_g", "ffn1_post_g", "ffn1_w_gate", "ffn1_w_up", "ffn1_w_down", "mix_pre_g", "mix_post_g",
             "w_in", "conv_w", "conv_b", "dt_bias", "a_log", "d_skip", "ssd_norm_g", "w_proj_ssd", "pool_mix",
             "pool_scale", "w_proj_pool", "w_out", "ffn2_pre_g", "ffn2_post_g", "ffn2_w_gate", "ffn2_w_up",
             "ffn2_w_down")
    stacked = (ffn1_pre_g, ffn1_post_g, ffn1_w_gate, ffn1_w_up, ffn1_w_down, mix_pre_g, mix_post_g, w_in, conv_w,
               conv_b, dt_bias, a_log, d_skip, ssd_norm_g, w_proj_ssd, pool_mix, pool_scale, w_proj_pool, w_out,
               ffn2_pre_g, ffn2_post_g, ffn2_w_gate, ffn2_w_up, ffn2_w_down)
    depth = w_in.shape[0]
    dec_b, dec_s, _ = x_sample.shape
    assert dec_s <= CHUNK
    y_p = x_prompt
    y_s = jnp.pad(x_sample, ((0, 0), (0, CHUNK - dec_s), (0, 0)))
    outs = [[] for _ in range(6)]
    for i in range(depth):
        q = _prep({k: v[i] for k, v in zip(names, stacked)})
        y_p, c1, s1, q1 = _layer(y_p, None, None, 0, q, tm=512)
        init = (jnp.pad(cache_conv[i], ((0, 0), (CONV_PAD - (CONV_K - 1), 0), (0, 0))),
                state_ssm[i].reshape(dec_b, D_INNER, STATE),
                jnp.pad(cache_pool[i], ((0, 0), (POOL_PAD - POOL_BUF, 0), (0, 0))))
        y_s, c2, s2, q2 = _layer(y_s, init, dec_s, PAST_LEN, q, tm=dec_b * CHUNK)
        for lst, v in zip(outs, (c1, s1, q1, c2, s2, q2)):
            lst.append(v)
    return (y_p, y_s[:, :dec_s]) + tuple(jnp.stack(v) for v in outs)
```

```python
import functools

import numpy as np
import jax
import jax.numpy as jnp
from jax import lax
from jax.experimental import pallas as pl
from jax.experimental.pallas import tpu as pltpu

F32 = jnp.float32
BF16 = jnp.bfloat16

D_MODEL = 1024
D_FF = 2816
D_INNER = 2048
HEADS = 32
HEAD_DIM = 64
GROUPS = 8
HPG = HEADS // GROUPS
GROUP_W = HPG * HEAD_DIM
STATE = 128
CONV_K = 4
CONV_DIM = D_INNER + 2 * GROUPS * STATE
D_POOL = 1024
POOL_WINDOWS = (2, 4, 8, 16)
POOL_GROUP_DIM = D_POOL // len(POOL_WINDOWS)
POOL_BUF = max(POOL_WINDOWS) - 1
PAST_LEN = 4096
EPS = 1e-6
CHUNK = 64

LANE = 128
DT_REP = 3
CONV_PAD = 8
POOL_PAD = 16
IN_CUTS = (0, D_INNER, D_INNER + CONV_DIM, D_INNER + CONV_DIM + D_POOL, D_INNER + CONV_DIM + D_POOL + D_MODEL,
           D_INNER + CONV_DIM + D_POOL + 2 * D_MODEL)
IN_CHUNK = 1536
N_IN_CHUNKS = IN_CUTS[-1] // IN_CHUNK
GROUP_COLS = 2 * GROUP_W + 2 * STATE
GROUPS_PER_CHUNK = IN_CHUNK // GROUP_COLS
MIX_ROWS = 256
VMEM_LIMIT = 56 * 1024 * 1024


def _const_spec(shape):
    nd = len(shape)
    return pl.BlockSpec(shape, lambda *_: (0,) * nd, pipeline_mode=pl.Buffered(1))


def _rms(x, g):
    ms = jnp.mean(x * x, axis=-1, keepdims=True)
    return x * lax.rsqrt(ms + EPS) * g


def _dot(a, b):
    return jnp.dot(a, b, preferred_element_type=F32)


def _silu(x):
    return x * jax.nn.sigmoid(x)


def _softplus(v):
    e = jnp.exp(-jnp.abs(v))
    u = 1.0 + e
    tiny = u == 1.0
    return jnp.maximum(v, 0.0) + jnp.where(tiny, e, jnp.log(u) * e / jnp.where(tiny, 1.0, u - 1.0))


def _ffn_kernel(x_ref, pre_ref, wg_ref, wu_ref, wd_ref, post_ref, o_ref):
    x = x_ref[...]
    h = _rms(x, pre_ref[...]).astype(BF16)
    g = _dot(h, wg_ref[...])
    u = _dot(h, wu_ref[...])
    a = (_silu(g) * u).astype(BF16)
    f = _dot(a, wd_ref[...])
    o_ref[...] = x + 0.5 * _rms(f, post_ref[...])


def _ffn(x, pre_g, wg, wu, wd, post_g, tm):
    n = x.shape[0]
    row = pl.BlockSpec((tm, D_MODEL), lambda i: (i, 0))
    return pl.pallas_call(
        _ffn_kernel,
        grid=(n // tm,),
        in_specs=[row, _const_spec((1, D_MODEL)), _const_spec((D_MODEL, D_FF)), _const_spec((D_MODEL, D_FF)),
                  _const_spec((D_FF, D_MODEL)), _const_spec((1, D_MODEL))],
        out_specs=row,
        out_shape=jax.ShapeDtypeStruct((n, D_MODEL), F32),
        compiler_params=pltpu.CompilerParams(dimension_semantics=("parallel",), vmem_limit_bytes=VMEM_LIMIT),
        name="ffn",
    )(x, pre_g, wg, wu, wd, post_g)


def _expand_heads(v3, e3):
    hi = v3.astype(BF16).astype(F32)
    r1 = v3 - hi
    mid = r1.astype(BF16).astype(F32)
    lo = r1 - mid
    lane = lax.broadcasted_iota(jnp.int32, v3.shape, 1)
    packed = jnp.where(lane < HEADS, hi, jnp.where(lane < 2 * HEADS, mid, jnp.where(lane < 3 * HEADS, lo, 0.0)))
    return _dot(packed.astype(BF16), e3)


def _in_col_ranges():
    ranges = []
    for g in range(GROUPS):
        ranges.append((g * GROUP_W, (g + 1) * GROUP_W))
        ranges.append((IN_CUTS[1] + g * GROUP_W, IN_CUTS[1] + (g + 1) * GROUP_W))
        ranges.append((IN_CUTS[1] + D_INNER + g * STATE, IN_CUTS[1] + D_INNER + (g + 1) * STATE))
        c0 = IN_CUTS[1] + D_INNER + GROUPS * STATE
        ranges.append((c0 + g * STATE, c0 + (g + 1) * STATE))
    ranges.append((IN_CUTS[2], IN_CUTS[5]))
    return ranges


def _in_col_order():
    return np.concatenate([np.arange(lo, hi) for lo, hi in _in_col_ranges()])


def _in_col_dest(col):
    src = int(_in_col_order()[col])
    for kind, lo, hi in zip(("z", "xbc", "u", "ga", "gb"), IN_CUTS[:-1], IN_CUTS[1:]):
        if lo <= src < hi:
            return kind, src - lo
    raise ValueError(col)


def _mixin_kernel(x_ref, pre_ref, *rest, t_len, nt, valid, pos0, has_init):
    w_refs, rest = rest[:N_IN_CHUNKS], rest[N_IN_CHUNKS:]
    wdt_ref, convw_ref, convb_ref, dtb_ref, alog_ref, dskip_ref, ng_ref, e3_ref, bdmask_ref = rest[:9]
    rest = rest[9:]
    if has_init:
        conv0_ref, ssm0_ref, pool0_ref = rest[:3]
        rest = rest[3:]
    yn_ref, pooled_ref, ga_ref, gb_ref, nconv_ref, nssm_ref, npool_ref, xpad, upad, xc, zs, hst = rest
    t = pl.program_id(1)
    n_xslab = CONV_DIM // LANE
    n_uslab = D_POOL // LANE

    def lanes(c):
        return slice(c * LANE, (c + 1) * LANE)

    @pl.when(t == 0)
    def _():
        for c in range(n_xslab):
            xpad[c, 0:CONV_PAD, :] = conv0_ref[:, lanes(c)] if has_init else jnp.zeros((CONV_PAD, LANE), F32)
        for c in range(n_uslab):
            upad[c, 0:POOL_PAD, :] = pool0_ref[:, lanes(c)] if has_init else jnp.zeros((POOL_PAD, LANE), F32)
        for g in range(GROUPS):
            hst[g] = ssm0_ref[g * GROUP_W:(g + 1) * GROUP_W, :].T if has_init else jnp.zeros((STATE, GROUP_W), F32)

    h = _rms(x_ref[...], pre_ref[...]).astype(BF16)

    v = _dot(h, wdt_ref[...]) + dtb_ref[...]
    dt3 = _softplus(v)
    if valid < t_len:
        dt3 = jnp.where(lax.broadcasted_iota(jnp.int32, (t_len, LANE), 0) < valid, dt3, 0.0)
    a3 = -jnp.exp(alog_ref[...])
    r_i = lax.broadcasted_iota(jnp.int32, (t_len, t_len), 0)
    c_i = lax.broadcasted_iota(jnp.int32, (t_len, t_len), 1)
    tril = ((c_i <= r_i) & (c_i // CHUNK == r_i // CHUNK)).astype(F32)
    acs3 = jnp.dot(tril, dt3 * a3, precision=lax.Precision.HIGHEST, preferred_element_type=F32)
    e3 = e3_ref[...]
    dt_e = _expand_heads(dt3, e3)
    acs_e = _expand_heads(acs3, e3)

    li = lax.broadcasted_iota(jnp.int32, (CHUNK, GROUP_W), 0)
    s_of_lane = lax.broadcasted_iota(jnp.int32, (CHUNK, GROUP_W), 1) % HEAD_DIM
    diag = li == s_of_lane
    causal = li >= s_of_lane
    bdmask = bdmask_ref[...]
    b_off = D_INNER
    c_off = D_INNER + GROUPS * STATE

    def project(ci):
        o = _dot(h, w_refs[ci][...])
        for j in range(IN_CHUNK // LANE):
            kind, off = _in_col_dest(ci * IN_CHUNK + j * LANE)
            piece = o[:, lanes(j)]
            if kind == "z":
                zs[:, off:off + LANE] = _silu(piece)
            elif kind == "xbc":
                xpad[off // LANE, CONV_PAD:CONV_PAD + t_len, :] = piece
            elif kind == "u":
                upad[off // LANE, POOL_PAD:POOL_PAD + t_len, :] = piece
            elif kind == "ga":
                ga_ref[:, off:off + LANE] = piece
            else:
                gb_ref[:, off:off + LANE] = piece

    def conv_slab(c):
        conv = convb_ref[:, lanes(c)]
        for k in range(CONV_K):
            start = CONV_PAD - (CONV_K - 1) + k
            conv = conv + xpad[c, start:start + t_len, :] * convw_ref[k:k + 1, lanes(c)]
        xc[:, lanes(c)] = _silu(conv)

    def scan_group(g):
        ch = slice(g * GROUP_W, (g + 1) * GROUP_W)
        h_state = hst[g]
        for c in range(t_len // CHUNK):
            rows = slice(c * CHUNK, (c + 1) * CHUNK)
            ae = acs_e[rows, ch]
            acs_row = jnp.sum(jnp.where(diag, ae, 0.0), axis=0, keepdims=True)
            decay = jnp.exp(jnp.where(causal, ae - acs_row, -jnp.inf))
            bg = xc[rows, b_off + g * STATE:b_off + (g + 1) * STATE].astype(BF16)
            cg = xc[rows, c_off + g * STATE:c_off + (g + 1) * STATE].astype(BF16)
            scores = lax.dot_general(cg, jnp.concatenate([bg] * HPG, axis=0), (((1,), (1,)), ((), ())),
                                     preferred_element_type=F32)
            xg = xc[rows, ch]
            xdt = xg * dt_e[rows, ch]
            xdt_bd = jnp.concatenate([xdt.astype(BF16)] * HPG, axis=0) * bdmask
            y_diag = _dot((scores * decay).astype(BF16), xdt_bd)
            y_off = _dot(cg, h_state.astype(BF16)) * jnp.exp(ae)
            last = ae[CHUNK - 1:CHUNK, :]
            xw = (xdt * jnp.exp(last - ae)).astype(BF16)
            h_state = jnp.exp(last) * h_state + lax.dot_general(bg, xw, (((0,), (0,)), ((), ())),
                                                                preferred_element_type=F32)
            y = (y_diag + y_off + dskip_ref[:, ch] * xg) * zs[rows, ch]
            ms = jnp.mean(y * y, axis=-1, keepdims=True)
            yn_ref[rows, ch] = (y * lax.rsqrt(ms + EPS) * ng_ref[:, ch]).astype(BF16)
        hst[g] = h_state

    project(0)
    for ci in range(1, N_IN_CHUNKS):
        project(ci)
        for g in range((ci - 1) * GROUPS_PER_CHUNK, min(ci * GROUPS_PER_CHUNK, GROUPS)):
            for c in (2 * g, 2 * g + 1, b_off // LANE + g, c_off // LANE + g):
                conv_slab(c)
            scan_group(g)

    pos = pos0 + t * t_len + lax.broadcasted_iota(jnp.int32, (t_len, 1), 0)
    for c in range(n_uslab):
        k = POOL_WINDOWS[c * LANE // POOL_GROUP_DIM]
        wsum = upad[c, POOL_PAD - (k - 1):POOL_PAD - (k - 1) + t_len, :]
        for j in range(k - 2, -1, -1):
            wsum = wsum + upad[c, POOL_PAD - j:POOL_PAD - j + t_len, :]
        cnt = jnp.minimum(pos + 1, k).astype(F32)
        pooled_ref[:, lanes(c)] = (wsum / cnt - upad[c, POOL_PAD:POOL_PAD + t_len, :]).astype(BF16)

    @pl.when(t == nt - 1)
    def _():
        for c in range(n_xslab):
            nconv_ref[:, lanes(c)] = xpad[c, CONV_PAD + valid - (CONV_K - 1):CONV_PAD + valid, :]
        for c in range(n_uslab):
            npool_ref[:, lanes(c)] = upad[c, POOL_PAD + valid - POOL_BUF:POOL_PAD + valid, :]
        for g in range(GROUPS):
            nssm_ref[g * GROUP_W:(g + 1) * GROUP_W, :] = hst[g].T

    @pl.when(t < nt - 1)
    def _():
        for c in range(n_xslab):
            xpad[c, 0:CONV_PAD, :] = xpad[c, t_len:t_len + CONV_PAD, :]
        for c in range(n_uslab):
            upad[c, 0:POOL_PAD, :] = upad[c, t_len:t_len + POOL_PAD, :]


def _mixin(x, q, init, valid, pos0):
    b, s, _ = x.shape
    t_len = min(s, MIX_ROWS)
    nt = s // t_len
    valid = t_len if valid is None else valid
    assert s % t_len == 0 and t_len % CHUNK == 0 and (valid == t_len or nt == 1)

    def seq(w):
        return pl.BlockSpec((None, t_len, w), lambda i, t: (i, t, 0))

    def per_b(r, w):
        return pl.BlockSpec((None, r, w), lambda i, t: (i, 0, 0))

    in_specs = ([seq(D_MODEL), _const_spec((1, D_MODEL))] + [_const_spec((D_MODEL, IN_CHUNK))] * N_IN_CHUNKS +
                [_const_spec((D_MODEL, LANE)), _const_spec((CONV_K, CONV_DIM)), _const_spec((1, CONV_DIM)),
                 _const_spec((1, LANE)), _const_spec((1, LANE)), _const_spec((1, D_INNER)),
                 _const_spec((1, D_INNER)), _const_spec((LANE, D_INNER)), _const_spec((HPG * CHUNK, GROUP_W))])
    args = [x, q["mix_pre_g"]] + q["in_w"] + [q["in_wdt"], q["conv_w"], q["conv_b"], q["dtb3"], q["alog3"],
                                             q["dskip_e"], q["ssd_norm_g"], q["e3"], q["bdmask"]]
    if init is not None:
        in_specs += [per_b(CONV_PAD, CONV_DIM), per_b(D_INNER, STATE), per_b(POOL_PAD, D_POOL)]
        args += list(init)
    return pl.pallas_call(
        functools.partial(_mixin_kernel, t_len=t_len, nt=nt, valid=valid, pos0=pos0, has_init=init is not None),
        grid=(b, nt),
        in_specs=in_specs,
        out_specs=[seq(D_INNER), seq(D_POOL), seq(D_MODEL), seq(D_MODEL), per_b(CONV_K - 1, CONV_DIM),
                   per_b(D_INNER, STATE), per_b(POOL_BUF, D_POOL)],
        out_shape=[jax.ShapeDtypeStruct((b, s, D_INNER), BF16), jax.ShapeDtypeStruct((b, s, D_POOL), BF16),
                   jax.ShapeDtypeStruct((b, s, D_MODEL), F32), jax.ShapeDtypeStruct((b, s, D_MODEL), F32),
                   jax.ShapeDtypeStruct((b, CONV_K - 1, CONV_DIM), F32),
                   jax.ShapeDtypeStruct((b, D_INNER, STATE), F32),
                   jax.ShapeDtypeStruct((b, POOL_BUF, D_POOL), F32)],
        scratch_shapes=[pltpu.VMEM((CONV_DIM // LANE, CONV_PAD + t_len, LANE), F32),
                        pltpu.VMEM((D_POOL // LANE, POOL_PAD + t_len, LANE), F32),
                        pltpu.VMEM((t_len, CONV_DIM), F32), pltpu.VMEM((t_len, D_INNER), F32),
                        pltpu.VMEM((GROUPS, STATE, GROUP_W), F32)],
        compiler_params=pltpu.CompilerParams(dimension_semantics=("parallel", "arbitrary"),
                                             vmem_limit_bytes=VMEM_LIMIT),
        name="mixin",
    )(*args)


def _outproj_kernel(x_ref, yn_ref, pooled_ref, ga_ref, gb_ref, wps_ref, mix_ref, pscale_ref, wpp_ref, wout_ref,
                    post_ref, o_ref):
    y_a = _dot(yn_ref[...], wps_ref[...])
    pooled = pooled_ref[...]
    mixed = jnp.concatenate(
        [_dot(pooled[:, gi * POOL_GROUP_DIM:(gi + 1) * POOL_GROUP_DIM], mix_ref[gi])
         for gi in range(len(POOL_WINDOWS))], axis=-1)
    y_b = _dot((mixed * pscale_ref[...]).astype(BF16), wpp_ref[...])
    merged = jax.nn.sigmoid(ga_ref[...]) * y_a + jax.nn.sigmoid(gb_ref[...]) * y_b
    m = _dot(merged.astype(BF16), wout_ref[...])
    o_ref[...] = x_ref[...] + _rms(m, post_ref[...])


def _outproj(x, yn, pooled, ga, gb, wps, mix, pscale, wpp, wout, post_g, tm):
    n = x.shape[0]

    def row(w):
        return pl.BlockSpec((tm, w), lambda i: (i, 0))

    return pl.pallas_call(
        _outproj_kernel,
        grid=(n // tm,),
        in_specs=[row(D_MODEL), row(D_INNER), row(D_POOL), row(D_MODEL), row(D_MODEL),
                  _const_spec((D_INNER, D_MODEL)),
                  _const_spec((len(POOL_WINDOWS), POOL_GROUP_DIM, POOL_GROUP_DIM)), _const_spec((1, D_POOL)),
                  _const_spec((D_POOL, D_MODEL)), _const_spec((D_MODEL, D_MODEL)), _const_spec((1, D_MODEL))],
        out_specs=row(D_MODEL),
        out_shape=jax.ShapeDtypeStruct((n, D_MODEL), F32),
        compiler_params=pltpu.CompilerParams(dimension_semantics=("parallel",), vmem_limit_bytes=VMEM_LIMIT),
        name="outproj",
    )(x, yn, pooled, ga, gb, wps, mix, pscale, wpp, wout, post_g)


def _head_expansion():
    e = np.zeros((LANE, HEADS * HEAD_DIM), np.float32)
    for j in range(DT_REP):
        for h in range(HEADS):
            e[j * HEADS + h, h * HEAD_DIM:(h + 1) * HEAD_DIM] = 1.0
    return jnp.asarray(e, BF16)


def _head_blockdiag():
    r = np.arange(HPG * CHUNK)[:, None] // CHUNK
    c = np.arange(GROUP_W)[None, :] // HEAD_DIM
    return jnp.asarray((r == c).astype(np.float32), BF16)


def _rep_heads(v):
    return jnp.concatenate([v] * DT_REP + [jnp.zeros((LANE - DT_REP * HEADS,), F32)])[None, :]


def _prep(p):
    w_in = p["w_in"]
    cuts = np.cumsum([0, D_INNER, CONV_DIM, HEADS, D_POOL, D_MODEL, D_MODEL]).tolist()
    wz, wxbc, wdt, wu, wga, wgb = [w_in[:, a:b] for a, b in zip(cuts[:-1], cuts[1:])]
    wdt3 = jnp.concatenate([wdt] * DT_REP + [jnp.zeros((D_MODEL, LANE - DT_REP * HEADS), F32)], axis=1)
    w_all = jnp.concatenate([wz, wxbc, wu, wga, wgb], axis=1)
    w_main = jnp.concatenate([w_all[:, lo:hi] for lo, hi in _in_col_ranges()], axis=1).astype(BF16)
    q = {k: v for k, v in p.items()}
    for k in ("ffn1_w_gate", "ffn1_w_up", "ffn1_w_down", "ffn2_w_gate", "ffn2_w_up", "ffn2_w_down",
              "w_proj_ssd", "pool_mix", "w_proj_pool", "w_out"):
        q[k] = p[k].astype(BF16)
    for k in ("ffn1_pre_g", "ffn1_post_g", "mix_pre_g", "mix_post_g", "ffn2_pre_g", "ffn2_post_g", "conv_b",
              "ssd_norm_g", "pool_scale"):
        q[k] = p[k][None, :]
    q["in_w"] = [w_main[:, i * IN_CHUNK:(i + 1) * IN_CHUNK] for i in range(N_IN_CHUNKS)]
    q["in_wdt"] = wdt3.astype(BF16)
    q["dtb3"] = _rep_heads(p["dt_bias"])
    q["alog3"] = _rep_heads(p["a_log"])
    q["dskip_e"] = jnp.repeat(p["d_skip"], HEAD_DIM)[None, :]
    q["e3"] = _head_expansion()
    q["bdmask"] = _head_blockdiag()
    return q


def _layer(x, init, valid, pos0, q, tm):
    b, s, _ = x.shape
    n = b * s
    xf = x.reshape(n, D_MODEL)
    xf = _ffn(xf, q["ffn1_pre_g"], q["ffn1_w_gate"], q["ffn1_w_up"], q["ffn1_w_down"], q["ffn1_post_g"], tm)
    yn, pooled, ga, gb, nconv, nssm, npool = _mixin(xf.reshape(b, s, D_MODEL), q, init, valid, pos0)
    xf = _outproj(xf, yn.reshape(n, D_INNER), pooled.reshape(n, D_POOL), ga.reshape(n, D_MODEL),
                  gb.reshape(n, D_MODEL), q["w_proj_ssd"], q["pool_mix"], q["pool_scale"], q["w_proj_pool"],
                  q["w_out"], q["mix_post_g"], tm)
    xf = _ffn(xf, q["ffn2_pre_g"], q["ffn2_w_gate"], q["ffn2_w_up"], q["ffn2_w_down"], q["ffn2_post_g"], tm)
    return xf.reshape(b, s, D_MODEL), nconv, nssm.reshape(b, HEADS, HEAD_DIM, STATE), npool


def kernel(x_prompt, x_sample, cache_conv, state_ssm, cache_pool, ffn1_pre_g, ffn1_post_g, ffn1_w_gate, ffn1_w_up, ffn1_w_down, mix_pre_g, mix_post_g, w_in, conv_w, conv_b, dt_bias, a_log, d_skip, ssd_norm_g, w_proj_ssd, pool_mix, pool_scale, w_proj_pool, w_out, ffn2_pre_g, ffn2_post_g, ffn2_w_gate, ffn2_w_up, ffn2_w_down):
    names = ("ffn1_pre_g", "ffn1_post_g", "ffn1_w_gate", "ffn1_w_up", "ffn1_w_down", "mix_pre_g", "mix_post_g",
             "w_in", "conv_w", "conv_b", "dt_bias", "a_log", "d_skip", "ssd_norm_g", "w_proj_ssd", "pool_mix",
             "pool_scale", "w_proj_pool", "w_out", "ffn2_pre_g", "ffn2_post_g", "ffn2_w_gate", "ffn2_w_up",
             "ffn2_w_down")
    stacked = (ffn1_pre_g, ffn1_post_g, ffn1_w_gate, ffn1_w_up, ffn1_w_down, mix_pre_g, mix_post_g, w_in, conv_w,
               conv_b, dt_bias, a_log, d_skip, ssd_norm_g, w_proj_ssd, pool_mix, pool_scale, w_proj_pool, w_out,
               ffn2_pre_g, ffn2_post_g, ffn2_w_gate, ffn2_w_up, ffn2_w_down)
    depth = w_in.shape[0]
    dec_b, dec_s, _ = x_sample.shape
    assert dec_s <= CHUNK
    y_p = x_prompt
    y_s = jnp.pad(x_sample, ((0, 0), (0, CHUNK - dec_s), (0, 0)))
    outs = [[] for _ in range(6)]
    for i in range(depth):
        q = _prep({k: v[i] for k, v in zip(names, stacked)})
        y_p, c1, s1, q1 = _layer(y_p, None, None, 0, q, tm=512)
        init = (jnp.pad(cache_conv[i], ((0, 0), (CONV_PAD - (CONV_K - 1), 0), (0, 0))),
                state_ssm[i].reshape(dec_b, D_INNER, STATE),
                jnp.pad(cache_pool[i], ((0, 0), (POOL_PAD - POOL_BUF, 0), (0, 0))))
        y_s, c2, s2, q2 = _layer(y_s, init, dec_s, PAST_LEN, q, tm=dec_b * CHUNK)
        for lst, v in zip(outs, (c1, s1, q1, c2, s2, q2)):
            lst.append(v)
    return (y_p, y_s[:, :dec_s]) + tuple(jnp.stack(v) for v in outs)
```

```python
import functools

import numpy as np
import jax
import jax.numpy as jnp
from jax import lax
from jax.experimental import pallas as pl
from jax.experimental.pallas import tpu as pltpu

F32 = jnp.float32
BF16 = jnp.bfloat16

D_MODEL = 1024
D_FF = 2816
D_INNER = 2048
HEADS = 32
HEAD_DIM = 64
GROUPS = 8
HPG = HEADS // GROUPS
GROUP_W = HPG * HEAD_DIM
STATE = 128
CONV_K = 4
CONV_DIM = D_INNER + 2 * GROUPS * STATE
D_POOL = 1024
POOL_WINDOWS = (2, 4, 8, 16)
POOL_GROUP_DIM = D_POOL // len(POOL_WINDOWS)
POOL_BUF = max(POOL_WINDOWS) - 1
PAST_LEN = 4096
EPS = 1e-6
CHUNK = 64

LANE = 128
DT_REP = 3
CONV_PAD = 8
POOL_PAD = 16
IN_CUTS = (0, D_INNER, D_INNER + CONV_DIM, D_INNER + CONV_DIM + D_POOL, D_INNER + CONV_DIM + D_POOL + D_MODEL,
           D_INNER + CONV_DIM + D_POOL + 2 * D_MODEL)
IN_CHUNK = 1536
N_IN_CHUNKS = IN_CUTS[-1] // IN_CHUNK
GROUP_COLS = 2 * GROUP_W + 2 * STATE
GROUPS_PER_CHUNK = IN_CHUNK // GROUP_COLS
FFN_ROWS = 1024
OUT_CHUNK = 512
OUT_SPLIT = D_MODEL // OUT_CHUNK
OUT_LAG = 2
MIX_ROWS = 256
VMEM_LIMIT = 56 * 1024 * 1024


def _const_spec(shape):
    nd = len(shape)
    return pl.BlockSpec(shape, lambda *_: (0,) * nd, pipeline_mode=pl.Buffered(1))


def _rms(x, g):
    ms = jnp.mean(x * x, axis=-1, keepdims=True)
    return x * lax.rsqrt(ms + EPS) * g


def _dot(a, b):
    return jnp.dot(a, b, preferred_element_type=F32)


def _silu(x):
    return x * jax.nn.sigmoid(x)


def _softplus(v):
    e = jnp.exp(-jnp.abs(v))
    u = 1.0 + e
    tiny = u == 1.0
    return jnp.maximum(v, 0.0) + jnp.where(tiny, e, jnp.log(u) * e / jnp.where(tiny, 1.0, u - 1.0))


def _ffn_kernel(x_ref, pre_ref, wg_ref, wu_ref, wd_ref, post_ref, o_ref):
    x = x_ref[...]
    h = _rms(x, pre_ref[...]).astype(BF16)
    g = _dot(h, wg_ref[...])
    u = _dot(h, wu_ref[...])
    a = (_silu(g) * u).astype(BF16)
    f = _dot(a, wd_ref[...])
    o_ref[...] = x + 0.5 * _rms(f, post_ref[...])


def _ffn(x, pre_g, wg, wu, wd, post_g, tm):
    n = x.shape[0]
    row = pl.BlockSpec((tm, D_MODEL), lambda i: (i, 0))
    return pl.pallas_call(
        _ffn_kernel,
        grid=(n // tm,),
        in_specs=[row, _const_spec((1, D_MODEL)), _const_spec((D_MODEL, D_FF)), _const_spec((D_MODEL, D_FF)),
                  _const_spec((D_FF, D_MODEL)), _const_spec((1, D_MODEL))],
        out_specs=row,
        out_shape=jax.ShapeDtypeStruct((n, D_MODEL), F32),
        compiler_params=pltpu.CompilerParams(dimension_semantics=("parallel",), vmem_limit_bytes=VMEM_LIMIT),
        name="ffn",
    )(x, pre_g, wg, wu, wd, post_g)


def _expand_heads(v3, e3):
    return _dot(_pack_parts(v3), e3)


def _pack_parts(v3):
    hi = v3.astype(BF16).astype(F32)
    r1 = v3 - hi
    mid = r1.astype(BF16).astype(F32)
    lo = r1 - mid
    lane = lax.broadcasted_iota(jnp.int32, v3.shape, 1)
    packed = jnp.where(lane < HEADS, hi, jnp.where(lane < 2 * HEADS, mid, jnp.where(lane < 3 * HEADS, lo, 0.0)))
    return packed.astype(BF16)


def _in_col_ranges():
    ranges = []
    for g in range(GROUPS):
        ranges.append((g * GROUP_W, (g + 1) * GROUP_W))
        ranges.append((IN_CUTS[1] + g * GROUP_W, IN_CUTS[1] + (g + 1) * GROUP_W))
        ranges.append((IN_CUTS[1] + D_INNER + g * STATE, IN_CUTS[1] + D_INNER + (g + 1) * STATE))
        c0 = IN_CUTS[1] + D_INNER + GROUPS * STATE
        ranges.append((c0 + g * STATE, c0 + (g + 1) * STATE))
    ranges.append((IN_CUTS[2], IN_CUTS[5]))
    return ranges


def _in_col_order():
    return np.concatenate([np.arange(lo, hi) for lo, hi in _in_col_ranges()])


def _in_col_dest(col):
    src = int(_in_col_order()[col])
    for kind, lo, hi in zip(("z", "xbc", "u", "ga", "gb"), IN_CUTS[:-1], IN_CUTS[1:]):
        if lo <= src < hi:
            return kind, src - lo
    raise ValueError(col)


def _mixer_kernel(x_ref, pre_ref, *rest, t_len, nt, valid, pos0, has_init):
    w_refs, rest = rest[:N_IN_CHUNKS], rest[N_IN_CHUNKS:]
    wdt_ref, convw_ref, convb_ref, dtb_ref, alog_ref, dskip_ref, ng_ref, e3_ref, bdmask_ref = rest[:9]
    rest = rest[9:]
    wps_refs, rest = rest[:OUT_SPLIT], rest[OUT_SPLIT:]
    mix_ref, pscale_ref = rest[:2]
    wpp_refs, rest = rest[2:2 + OUT_SPLIT], rest[2 + OUT_SPLIT:]
    wout_refs, rest = rest[:OUT_SPLIT], rest[OUT_SPLIT:]
    post_ref, rest = rest[0], rest[1:]
    if has_init:
        conv0_ref, ssm0_ref, pool0_ref = rest[:3]
        rest = rest[3:]
    o_ref, nconv_ref, nssm_ref, npool_ref, xpad, upad, xc, zs, hst, yn_s, ya_s, ga_s, gb_s = rest
    t = pl.program_id(1)
    n_xslab = CONV_DIM // LANE
    n_uslab = D_POOL // LANE

    def lanes(c):
        return slice(c * LANE, (c + 1) * LANE)

    @pl.when(t == 0)
    def _():
        for c in range(n_xslab):
            xpad[c, 0:CONV_PAD, :] = conv0_ref[:, lanes(c)] if has_init else jnp.zeros((CONV_PAD, LANE), F32)
        for c in range(n_uslab):
            upad[c, 0:POOL_PAD, :] = pool0_ref[:, lanes(c)] if has_init else jnp.zeros((POOL_PAD, LANE), F32)
        for g in range(GROUPS):
            hst[g] = ssm0_ref[g * GROUP_W:(g + 1) * GROUP_W, :].T if has_init else jnp.zeros((STATE, GROUP_W), F32)

    h = _rms(x_ref[...], pre_ref[...]).astype(BF16)

    v = _dot(h, wdt_ref[...]) + dtb_ref[...]
    dt3 = _softplus(v)
    if valid < t_len:
        dt3 = jnp.where(lax.broadcasted_iota(jnp.int32, (t_len, LANE), 0) < valid, dt3, 0.0)
    a3 = -jnp.exp(alog_ref[...])
    r_i = lax.broadcasted_iota(jnp.int32, (t_len, t_len), 0)
    c_i = lax.broadcasted_iota(jnp.int32, (t_len, t_len), 1)
    tril = ((c_i <= r_i) & (c_i // CHUNK == r_i // CHUNK)).astype(BF16)
    part = _dot(tril, _pack_parts(dt3 * a3))
    acs3 = part
    for j in range(1, LANE // HEADS):
        acs3 = acs3 + pltpu.roll(part, j * HEADS, 1)

    li = lax.broadcasted_iota(jnp.int32, (CHUNK, GROUP_W), 0)
    s_of_lane = lax.broadcasted_iota(jnp.int32, (CHUNK, GROUP_W), 1) % HEAD_DIM
    diag = li == s_of_lane
    causal = li >= s_of_lane
    bdmask = bdmask_ref[...]
    b_off = D_INNER
    c_off = D_INNER + GROUPS * STATE

    def project(ci):
        o = _dot(h, w_refs[ci][...])
        for j in range(IN_CHUNK // LANE):
            kind, off = _in_col_dest(ci * IN_CHUNK + j * LANE)
            piece = o[:, lanes(j)]
            if kind == "z":
                zs[:, off:off + LANE] = _silu(piece)
            elif kind == "xbc":
                xpad[off // LANE, CONV_PAD:CONV_PAD + t_len, :] = piece
            elif kind == "u":
                upad[off // LANE, POOL_PAD:POOL_PAD + t_len, :] = piece
            elif kind == "ga":
                ga_s[:, off:off + LANE] = piece
            else:
                gb_s[:, off:off + LANE] = piece

    def conv_slab(c):
        conv = convb_ref[:, lanes(c)]
        for k in range(CONV_K):
            start = CONV_PAD - (CONV_K - 1) + k
            conv = conv + xpad[c, start:start + t_len, :] * convw_ref[k:k + 1, lanes(c)]
        xc[:, lanes(c)] = _silu(conv)

    def scan_group(g):
        ch = slice(g * GROUP_W, (g + 1) * GROUP_W)
        h_state = hst[g]
        for c in range(t_len // CHUNK):
            rows = slice(c * CHUNK, (c + 1) * CHUNK)
            ae = acs_e[rows, ch]
            acs_row = jnp.sum(jnp.where(diag, ae, 0.0), axis=0, keepdims=True)
            decay = jnp.exp(jnp.where(causal, ae - acs_row, -jnp.inf))
            bg = xc[rows, b_off + g * STATE:b_off + (g + 1) * STATE].astype(BF16)
            cg = xc[rows, c_off + g * STATE:c_off + (g + 1) * STATE].astype(BF16)
            scores = lax.dot_general(cg, jnp.concatenate([bg] * HPG, axis=0), (((1,), (1,)), ((), ())),
                                     preferred_element_type=F32)
            xg = xc[rows, ch]
            xdt = xg * dt_e[rows, ch]
            xdt_bd = jnp.concatenate([xdt.astype(BF16)] * HPG, axis=0) * bdmask
            y_diag = _dot((scores * decay).astype(BF16), xdt_bd)
            y_off = _dot(cg, h_state.astype(BF16)) * jnp.exp(ae)
            last = ae[CHUNK - 1:CHUNK, :]
            xw = (xdt * jnp.exp(last - ae)).astype(BF16)
            h_state = jnp.exp(last) * h_state + lax.dot_general(bg, xw, (((0,), (0,)), ((), ())),
                                                                preferred_element_type=F32)
            y = (y_diag + y_off + dskip_ref[:, ch] * xg) * zs[rows, ch]
            ms = jnp.mean(y * y, axis=-1, keepdims=True)
            yn_s[rows, ch] = (y * lax.rsqrt(ms + EPS) * ng_ref[:, ch]).astype(BF16)
        hst[g] = h_state

    def out_part(g):
        ch = slice(g * GROUP_W, (g + 1) * GROUP_W)
        for j, w_ref in enumerate(wps_refs):
            part = _dot(yn_s[:, ch], w_ref[ch, :])
            cols = slice(j * OUT_CHUNK, (j + 1) * OUT_CHUNK)
            ya_s[:, cols] = part if g == 0 else ya_s[:, cols] + part

    project(0)
    e3 = e3_ref[...]
    dt_e = _expand_heads(dt3, e3)
    acs_e = _expand_heads(acs3, e3)
    for ci in range(1, N_IN_CHUNKS):
        project(ci)
        for g in range((ci - 1) * GROUPS_PER_CHUNK, min(ci * GROUPS_PER_CHUNK, GROUPS)):
            for c in (2 * g, 2 * g + 1, b_off // LANE + g, c_off // LANE + g):
                conv_slab(c)
            scan_group(g)
            if g >= OUT_LAG:
                out_part(g - OUT_LAG)
    for g in range(GROUPS - OUT_LAG, GROUPS):
        out_part(g)

    pos = pos0 + t * t_len + lax.broadcasted_iota(jnp.int32, (t_len, 1), 0)
    pooled = []
    for c in range(n_uslab):
        k = POOL_WINDOWS[c * LANE // POOL_GROUP_DIM]
        wsum = upad[c, POOL_PAD - (k - 1):POOL_PAD - (k - 1) + t_len, :]
        for j in range(k - 2, -1, -1):
            wsum = wsum + upad[c, POOL_PAD - j:POOL_PAD - j + t_len, :]
        cnt = jnp.minimum(pos + 1, k).astype(F32)
        pooled.append((wsum / cnt - upad[c, POOL_PAD:POOL_PAD + t_len, :]).astype(BF16))
    slabs_per_group = POOL_GROUP_DIM // LANE
    mixed = jnp.concatenate(
        [_dot(jnp.concatenate(pooled[gi * slabs_per_group:(gi + 1) * slabs_per_group], axis=-1), mix_ref[gi])
         for gi in range(len(POOL_WINDOWS))], axis=-1)
    mixed = (mixed * pscale_ref[...]).astype(BF16)
    y_b = jnp.concatenate([_dot(mixed, w_ref[...]) for w_ref in wpp_refs], axis=-1)

    merged = (jax.nn.sigmoid(ga_s[...]) * ya_s[...] + jax.nn.sigmoid(gb_s[...]) * y_b).astype(BF16)
    m = jnp.concatenate([_dot(merged, w_ref[...]) for w_ref in wout_refs], axis=-1)
    o_ref[...] = x_ref[...] + _rms(m, post_ref[...])

    @pl.when(t == nt - 1)
    def _():
        for c in range(n_xslab):
            nconv_ref[:, lanes(c)] = xpad[c, CONV_PAD + valid - (CONV_K - 1):CONV_PAD + valid, :]
        for c in range(n_uslab):
            npool_ref[:, lanes(c)] = upad[c, POOL_PAD + valid - POOL_BUF:POOL_PAD + valid, :]
        for g in range(GROUPS):
            nssm_ref[g * GROUP_W:(g + 1) * GROUP_W, :] = hst[g].T

    @pl.when(t < nt - 1)
    def _():
        for c in range(n_xslab):
            xpad[c, 0:CONV_PAD, :] = xpad[c, t_len:t_len + CONV_PAD, :]
        for c in range(n_uslab):
            upad[c, 0:POOL_PAD, :] = upad[c, t_len:t_len + POOL_PAD, :]


def _mixer(x, q, init, valid, pos0):
    b, s, _ = x.shape
    t_len = min(s, MIX_ROWS)
    nt = s // t_len
    valid = t_len if valid is None else valid
    assert s % t_len == 0 and t_len % CHUNK == 0 and (valid == t_len or nt == 1)

    def seq(w):
        return pl.BlockSpec((None, t_len, w), lambda i, t: (i, t, 0))

    def per_b(r, w):
        return pl.BlockSpec((None, r, w), lambda i, t: (i, 0, 0))

    in_specs = ([seq(D_MODEL), _const_spec((1, D_MODEL))] + [_const_spec((D_MODEL, IN_CHUNK))] * N_IN_CHUNKS +
                [_const_spec((D_MODEL, LANE)), _const_spec((CONV_K, CONV_DIM)), _const_spec((1, CONV_DIM)),
                 _const_spec((1, LANE)), _const_spec((1, LANE)), _const_spec((1, D_INNER)),
                 _const_spec((1, D_INNER)), _const_spec((LANE, D_INNER)), _const_spec((HPG * CHUNK, GROUP_W))] +
                [_const_spec((D_INNER, OUT_CHUNK))] * OUT_SPLIT +
                [_const_spec((len(POOL_WINDOWS), POOL_GROUP_DIM, POOL_GROUP_DIM)), _const_spec((1, D_POOL))] +
                [_const_spec((D_POOL, OUT_CHUNK))] * OUT_SPLIT + [_const_spec((D_MODEL, OUT_CHUNK))] * OUT_SPLIT +
                [_const_spec((1, D_MODEL))])
    args = ([x, q["mix_pre_g"]] + q["in_w"] +
            [q["in_wdt"], q["conv_w"], q["conv_b"], q["dtb3"], q["alog3"], q["dskip_e"], q["ssd_norm_g"], q["e3"],
             q["bdmask"]] + q["w_proj_ssd"] + [q["pool_mix"], q["pool_scale"]] + q["w_proj_pool"] + q["w_out"] +
            [q["mix_post_g"]])
    if init is not None:
        in_specs += [per_b(CONV_PAD, CONV_DIM), per_b(D_INNER, STATE), per_b(POOL_PAD, D_POOL)]
        args += list(init)
    return pl.pallas_call(
        functools.partial(_mixer_kernel, t_len=t_len, nt=nt, valid=valid, pos0=pos0, has_init=init is not None),
        grid=(b, nt),
        in_specs=in_specs,
        out_specs=[seq(D_MODEL), per_b(CONV_K - 1, CONV_DIM), per_b(D_INNER, STATE), per_b(POOL_BUF, D_POOL)],
        out_shape=[jax.ShapeDtypeStruct((b, s, D_MODEL), F32),
                   jax.ShapeDtypeStruct((b, CONV_K - 1, CONV_DIM), F32),
                   jax.ShapeDtypeStruct((b, D_INNER, STATE), F32),
                   jax.ShapeDtypeStruct((b, POOL_BUF, D_POOL), F32)],
        scratch_shapes=[pltpu.VMEM((CONV_DIM // LANE, CONV_PAD + t_len, LANE), F32),
                        pltpu.VMEM((D_POOL // LANE, POOL_PAD + t_len, LANE), F32),
                        pltpu.VMEM((t_len, CONV_DIM), F32), pltpu.VMEM((t_len, D_INNER), F32),
                        pltpu.VMEM((GROUPS, STATE, GROUP_W), F32), pltpu.VMEM((t_len, D_INNER), BF16),
                        pltpu.VMEM((t_len, D_MODEL), F32), pltpu.VMEM((t_len, D_MODEL), F32),
                        pltpu.VMEM((t_len, D_MODEL), F32)],
        compiler_params=pltpu.CompilerParams(dimension_semantics=("parallel", "arbitrary"),
                                             vmem_limit_bytes=VMEM_LIMIT),
        name="mixer",
    )(*args)


def _head_expansion():
    e = np.zeros((LANE, HEADS * HEAD_DIM), np.float32)
    for j in range(DT_REP):
        for h in range(HEADS):
            e[j * HEADS + h, h * HEAD_DIM:(h + 1) * HEAD_DIM] = 1.0
    return jnp.asarray(e, BF16)


def _head_blockdiag():
    r = np.arange(HPG * CHUNK)[:, None] // CHUNK
    c = np.arange(GROUP_W)[None, :] // HEAD_DIM
    return jnp.asarray((r == c).astype(np.float32), BF16)


def _rep_heads(v):
    return jnp.concatenate([v] * DT_REP + [jnp.zeros((LANE - DT_REP * HEADS,), F32)])[None, :]


def _prep(p):
    w_in = p["w_in"]
    cuts = np.cumsum([0, D_INNER, CONV_DIM, HEADS, D_POOL, D_MODEL, D_MODEL]).tolist()
    wdt = w_in[:, cuts[2]:cuts[3]]
    wdt3 = jnp.concatenate([wdt] * DT_REP + [jnp.zeros((D_MODEL, LANE - DT_REP * HEADS), F32)], axis=1)
    skip = [HEADS if lo >= IN_CUTS[2] else 0 for lo, _ in _in_col_ranges()]
    w_main = jnp.concatenate([w_in[:, lo + d:hi + d] for (lo, hi), d in zip(_in_col_ranges(), skip)],
                             axis=1).astype(BF16)
    q = {k: v for k, v in p.items()}
    for k in ("ffn1_w_gate", "ffn1_w_up", "ffn1_w_down", "ffn2_w_gate", "ffn2_w_up", "ffn2_w_down", "pool_mix"):
        q[k] = p[k].astype(BF16)
    for k in ("w_proj_ssd", "w_proj_pool", "w_out"):
        q[k] = [p[k][:, j * OUT_CHUNK:(j + 1) * OUT_CHUNK].astype(BF16) for j in range(OUT_SPLIT)]
    for k in ("ffn1_pre_g", "ffn1_post_g", "mix_pre_g", "mix_post_g", "ffn2_pre_g", "ffn2_post_g", "conv_b",
              "ssd_norm_g", "pool_scale"):
        q[k] = p[k][None, :]
    q["in_w"] = [w_main[:, i * IN_CHUNK:(i + 1) * IN_CHUNK] for i in range(N_IN_CHUNKS)]
    q["in_wdt"] = wdt3.astype(BF16)
    q["dtb3"] = _rep_heads(p["dt_bias"])
    q["alog3"] = _rep_heads(p["a_log"])
    q["dskip_e"] = jnp.repeat(p["d_skip"], HEAD_DIM)[None, :]
    q["e3"] = _head_expansion()
    q["bdmask"] = _head_blockdiag()
    return q


def _layer(x, init, valid, pos0, q):
    b, s, _ = x.shape
    n = b * s
    tm = min(n, FFN_ROWS)
    xf = x.reshape(n, D_MODEL)
    xf = _ffn(xf, q["ffn1_pre_g"], q["ffn1_w_gate"], q["ffn1_w_up"], q["ffn1_w_down"], q["ffn1_post_g"], tm)
    xm, nconv, nssm, npool = _mixer(xf.reshape(b, s, D_MODEL), q, init, valid, pos0)
    xf = xm.reshape(n, D_MODEL)
    xf = _ffn(xf, q["ffn2_pre_g"], q["ffn2_w_gate"], q["ffn2_w_up"], q["ffn2_w_down"], q["ffn2_post_g"], tm)
    return xf.reshape(b, s, D_MODEL), nconv, nssm.reshape(b, HEADS, HEAD_DIM, STATE), npool


def kernel(x_prompt, x_sample, cache_conv, state_ssm, cache_pool, ffn1_pre_g, ffn1_post_g, ffn1_w_gate, ffn1_w_up, ffn1_w_down, mix_pre_g, mix_post_g, w_in, conv_w, conv_b, dt_bias, a_log, d_skip, ssd_norm_g, w_proj_ssd, pool_mix, pool_scale, w_proj_pool, w_out, ffn2_pre_g, ffn2_post_g, ffn2_w_gate, ffn2_w_up, ffn2_w_down):
    names = ("ffn1_pre_g", "ffn1_post_g", "ffn1_w_gate", "ffn1_w_up", "ffn1_w_down", "mix_pre_g", "mix_post_g",
             "w_in", "conv_w", "conv_b", "dt_bias", "a_log", "d_skip", "ssd_norm_g", "w_proj_ssd", "pool_mix",
             "pool_scale", "w_proj_pool", "w_out", "ffn2_pre_g", "ffn2_post_g", "ffn2_w_gate", "ffn2_w_up",
             "ffn2_w_down")
    stacked = (ffn1_pre_g, ffn1_post_g, ffn1_w_gate, ffn1_w_up, ffn1_w_down, mix_pre_g, mix_post_g, w_in, conv_w,
               conv_b, dt_bias, a_log, d_skip, ssd_norm_g, w_proj_ssd, pool_mix, pool_scale, w_proj_pool, w_out,
               ffn2_pre_g, ffn2_post_g, ffn2_w_gate, ffn2_w_up, ffn2_w_down)
    depth = w_in.shape[0]
    dec_b, dec_s, _ = x_sample.shape
    assert dec_s <= CHUNK
    y_p = x_prompt
    y_s = jnp.pad(x_sample, ((0, 0), (0, CHUNK - dec_s), (0, 0)))
    outs = [[] for _ in range(6)]
    for i in range(depth):
        q = _prep({k: v[i] for k, v in zip(names, stacked)})
        y_p, c1, s1, q1 = _layer(y_p, None, None, 0, q)
        init = (jnp.pad(cache_conv[i], ((0, 0), (CONV_PAD - (CONV_K - 1), 0), (0, 0))),
                state_ssm[i].reshape(dec_b, D_INNER, STATE),
                jnp.pad(cache_pool[i], ((0, 0), (POOL_PAD - POOL_BUF, 0), (0, 0))))
        y_s, c2, s2, q2 = _layer(y_s, init, dec_s, PAST_LEN, q)
        for lst, v in zip(outs, (c1, s1, q1, c2, s2, q2)):
            lst.append(v)
    return (y_p, y_s[:, :dec_s]) + tuple(jnp.stack(v) for v in outs)
```

```python
import functools

import numpy as np
import jax
import jax.numpy as jnp
from jax import lax
from jax.experimental import pallas as pl
from jax.experimental.pallas import tpu as pltpu

F32 = jnp.float32
BF16 = jnp.bfloat16

D_MODEL = 1024
D_FF = 2816
D_INNER = 2048
HEADS = 32
HEAD_DIM = 64
GROUPS = 8
HPG = HEADS // GROUPS
GROUP_W = HPG * HEAD_DIM
STATE = 128
CONV_K = 4
CONV_DIM = D_INNER + 2 * GROUPS * STATE
D_POOL = 1024
POOL_WINDOWS = (2, 4, 8, 16)
POOL_GROUP_DIM = D_POOL // len(POOL_WINDOWS)
POOL_BUF = max(POOL_WINDOWS) - 1
PAST_LEN = 4096
EPS = 1e-6
CHUNK = 64

LANE = 128
DT_REP = 3
CONV_PAD = 8
POOL_PAD = 16
IN_CUTS = (0, D_INNER, D_INNER + CONV_DIM, D_INNER + CONV_DIM + D_POOL, D_INNER + CONV_DIM + D_POOL + D_MODEL,
           D_INNER + CONV_DIM + D_POOL + 2 * D_MODEL)
IN_CHUNK = 1536
N_IN_CHUNKS = IN_CUTS[-1] // IN_CHUNK
GROUP_COLS = 2 * GROUP_W + 2 * STATE
GROUPS_PER_CHUNK = IN_CHUNK // GROUP_COLS
FFN_ROWS = 1024
FFN_BLOCK = 256
OUT_CHUNK = 512
OUT_SPLIT = D_MODEL // OUT_CHUNK
OUT_LAG = 2
MIX_ROWS = 256
VMEM_LIMIT = 56 * 1024 * 1024


def _const_spec(shape):
    nd = len(shape)
    return pl.BlockSpec(shape, lambda *_: (0,) * nd, pipeline_mode=pl.Buffered(1))


def _rms(x, g):
    ms = jnp.mean(x * x, axis=-1, keepdims=True)
    return x * lax.rsqrt(ms + EPS) * g


def _dot(a, b):
    return jnp.dot(a, b, preferred_element_type=F32)


def _silu(x):
    return x * jax.nn.sigmoid(x)


def _softplus(v):
    e = jnp.exp(-jnp.abs(v))
    u = 1.0 + e
    tiny = u == 1.0
    return jnp.maximum(v, 0.0) + jnp.where(tiny, e, jnp.log(u) * e / jnp.where(tiny, 1.0, u - 1.0))


def _ffn_kernel(x_ref, pre_ref, wg_ref, wu_ref, wd_ref, post_ref, o_ref):
    for r in range(x_ref.shape[0] // FFN_BLOCK):
        rows = slice(r * FFN_BLOCK, (r + 1) * FFN_BLOCK)
        x = x_ref[rows, :]
        h = _rms(x, pre_ref[...]).astype(BF16)
        g = _dot(h, wg_ref[...])
        u = _dot(h, wu_ref[...])
        a = (_silu(g) * u).astype(BF16)
        f = _dot(a, wd_ref[...])
        o_ref[rows, :] = x + 0.5 * _rms(f, post_ref[...])


def _ffn(x, pre_g, wg, wu, wd, post_g, tm):
    n = x.shape[0]
    assert n % tm == 0 and tm % FFN_BLOCK == 0
    row = pl.BlockSpec((tm, D_MODEL), lambda i: (i, 0))
    return pl.pallas_call(
        _ffn_kernel,
        grid=(n // tm,),
        in_specs=[row, _const_spec((1, D_MODEL)), _const_spec((D_MODEL, D_FF)), _const_spec((D_MODEL, D_FF)),
                  _const_spec((D_FF, D_MODEL)), _const_spec((1, D_MODEL))],
        out_specs=row,
        out_shape=jax.ShapeDtypeStruct((n, D_MODEL), F32),
        compiler_params=pltpu.CompilerParams(dimension_semantics=("parallel",), vmem_limit_bytes=VMEM_LIMIT),
        name="ffn",
    )(x, pre_g, wg, wu, wd, post_g)


def _expand_heads(v3, e3):
    return _dot(_pack_parts(v3), e3)


def _pack_parts(v3):
    hi = v3.astype(BF16).astype(F32)
    r1 = v3 - hi
    mid = r1.astype(BF16).astype(F32)
    lo = r1 - mid
    lane = lax.broadcasted_iota(jnp.int32, v3.shape, 1)
    packed = jnp.where(lane < HEADS, hi, jnp.where(lane < 2 * HEADS, mid, jnp.where(lane < 3 * HEADS, lo, 0.0)))
    return packed.astype(BF16)


def _in_col_ranges():
    ranges = []
    for g in range(GROUPS):
        ranges.append((g * GROUP_W, (g + 1) * GROUP_W))
        ranges.append((IN_CUTS[1] + g * GROUP_W, IN_CUTS[1] + (g + 1) * GROUP_W))
        ranges.append((IN_CUTS[1] + D_INNER + g * STATE, IN_CUTS[1] + D_INNER + (g + 1) * STATE))
        c0 = IN_CUTS[1] + D_INNER + GROUPS * STATE
        ranges.append((c0 + g * STATE, c0 + (g + 1) * STATE))
    ranges.append((IN_CUTS[2], IN_CUTS[5]))
    return ranges


def _in_col_order():
    return np.concatenate([np.arange(lo, hi) for lo, hi in _in_col_ranges()])


def _in_col_dest(col):
    src = int(_in_col_order()[col])
    for kind, lo, hi in zip(("z", "xbc", "u", "ga", "gb"), IN_CUTS[:-1], IN_CUTS[1:]):
        if lo <= src < hi:
            return kind, src - lo
    raise ValueError(col)


def _mixer_kernel(x_ref, pre_ref, *rest, t_len, nt, valid, pos0, has_init):
    w_refs, rest = rest[:N_IN_CHUNKS], rest[N_IN_CHUNKS:]
    wdt_ref, convw_ref, convb_ref, dtb_ref, alog_ref, dskip_ref, ng_ref, e3_ref, bdmask_ref = rest[:9]
    rest = rest[9:]
    wps_refs, rest = rest[:OUT_SPLIT], rest[OUT_SPLIT:]
    mix_ref, pscale_ref = rest[:2]
    wpp_refs, rest = rest[2:2 + OUT_SPLIT], rest[2 + OUT_SPLIT:]
    wout_refs, rest = rest[:OUT_SPLIT], rest[OUT_SPLIT:]
    post_ref, rest = rest[0], rest[1:]
    if has_init:
        conv0_ref, ssm0_ref, pool0_ref = rest[:3]
        rest = rest[3:]
    o_ref, nconv_ref, nssm_ref, npool_ref, xpad, upad, xc, zs, hst, yn_s, ya_s, ga_s, gb_s = rest
    t = pl.program_id(1)
    n_xslab = CONV_DIM // LANE
    n_uslab = D_POOL // LANE

    def lanes(c):
        return slice(c * LANE, (c + 1) * LANE)

    @pl.when(t == 0)
    def _():
        for c in range(n_xslab):
            xpad[c, 0:CONV_PAD, :] = conv0_ref[:, lanes(c)] if has_init else jnp.zeros((CONV_PAD, LANE), F32)
        for c in range(n_uslab):
            upad[c, 0:POOL_PAD, :] = pool0_ref[:, lanes(c)] if has_init else jnp.zeros((POOL_PAD, LANE), F32)
        for g in range(GROUPS):
            hst[g] = ssm0_ref[g * GROUP_W:(g + 1) * GROUP_W, :].T if has_init else jnp.zeros((STATE, GROUP_W), F32)

    h = _rms(x_ref[...], pre_ref[...]).astype(BF16)

    v = _dot(h, wdt_ref[...]) + dtb_ref[...]
    dt3 = _softplus(v)
    if valid < t_len:
        dt3 = jnp.where(lax.broadcasted_iota(jnp.int32, (t_len, LANE), 0) < valid, dt3, 0.0)
    a3 = -jnp.exp(alog_ref[...])
    r_i = lax.broadcasted_iota(jnp.int32, (t_len, t_len), 0)
    c_i = lax.broadcasted_iota(jnp.int32, (t_len, t_len), 1)
    tril = ((c_i <= r_i) & (c_i // CHUNK == r_i // CHUNK)).astype(BF16)
    part = _dot(tril, _pack_parts(dt3 * a3))
    acs3 = part
    for j in range(1, LANE // HEADS):
        acs3 = acs3 + pltpu.roll(part, j * HEADS, 1)

    li = lax.broadcasted_iota(jnp.int32, (CHUNK, GROUP_W), 0)
    s_of_lane = lax.broadcasted_iota(jnp.int32, (CHUNK, GROUP_W), 1) % HEAD_DIM
    diag = li == s_of_lane
    causal = li >= s_of_lane
    bdmask = bdmask_ref[...]
    b_off = D_INNER
    c_off = D_INNER + GROUPS * STATE

    def project(ci):
        o = _dot(h, w_refs[ci][...])
        for j in range(IN_CHUNK // LANE):
            kind, off = _in_col_dest(ci * IN_CHUNK + j * LANE)
            piece = o[:, lanes(j)]
            if kind == "z":
                zs[:, off:off + LANE] = _silu(piece)
            elif kind == "xbc":
                xpad[off // LANE, CONV_PAD:CONV_PAD + t_len, :] = piece
            elif kind == "u":
                upad[off // LANE, POOL_PAD:POOL_PAD + t_len, :] = piece
            elif kind == "ga":
                ga_s[:, off:off + LANE] = piece
            else:
                gb_s[:, off:off + LANE] = piece

    def conv_slab(c):
        conv = convb_ref[:, lanes(c)]
        for k in range(CONV_K):
            start = CONV_PAD - (CONV_K - 1) + k
            conv = conv + xpad[c, start:start + t_len, :] * convw_ref[k:k + 1, lanes(c)]
        xc[:, lanes(c)] = _silu(conv)

    def scan_group(g):
        ch = slice(g * GROUP_W, (g + 1) * GROUP_W)
        h_state = hst[g]
        for c in range(t_len // CHUNK):
            rows = slice(c * CHUNK, (c + 1) * CHUNK)
            ae = acs_e[rows, ch]
            acs_row = jnp.sum(jnp.where(diag, ae, 0.0), axis=0, keepdims=True)
            decay = jnp.exp(jnp.where(causal, ae - acs_row, -jnp.inf))
            bg = xc[rows, b_off + g * STATE:b_off + (g + 1) * STATE].astype(BF16)
            cg = xc[rows, c_off + g * STATE:c_off + (g + 1) * STATE].astype(BF16)
            scores = lax.dot_general(cg, jnp.concatenate([bg] * HPG, axis=0), (((1,), (1,)), ((), ())),
                                     preferred_element_type=F32)
            xg = xc[rows, ch]
            xdt = xg * dt_e[rows, ch]
            xdt_bd = jnp.concatenate([xdt.astype(BF16)] * HPG, axis=0) * bdmask
            y_diag = _dot((scores * decay).astype(BF16), xdt_bd)
            y_off = _dot(cg, h_state.astype(BF16)) * jnp.exp(ae)
            last = ae[CHUNK - 1:CHUNK, :]
            xw = (xdt * jnp.exp(last - ae)).astype(BF16)
            h_state = jnp.exp(last) * h_state + lax.dot_general(bg, xw, (((0,), (0,)), ((), ())),
                                                                preferred_element_type=F32)
            y = (y_diag + y_off + dskip_ref[:, ch] * xg) * zs[rows, ch]
            ms = jnp.mean(y * y, axis=-1, keepdims=True)
            yn_s[rows, ch] = (y * lax.rsqrt(ms + EPS) * ng_ref[:, ch]).astype(BF16)
        hst[g] = h_state

    def out_part(g):
        ch = slice(g * GROUP_W, (g + 1) * GROUP_W)
        for j, w_ref in enumerate(wps_refs):
            part = _dot(yn_s[:, ch], w_ref[ch, :])
            cols = slice(j * OUT_CHUNK, (j + 1) * OUT_CHUNK)
            ya_s[:, cols] = part if g == 0 else ya_s[:, cols] + part

    project(0)
    e3 = e3_ref[...]
    dt_e = _expand_heads(dt3, e3)
    acs_e = _expand_heads(acs3, e3)
    for ci in range(1, N_IN_CHUNKS):
        project(ci)
        for g in range((ci - 1) * GROUPS_PER_CHUNK, min(ci * GROUPS_PER_CHUNK, GROUPS)):
            for c in (2 * g, 2 * g + 1, b_off // LANE + g, c_off // LANE + g):
                conv_slab(c)
            scan_group(g)
            if g >= OUT_LAG:
                out_part(g - OUT_LAG)
    for g in range(GROUPS - OUT_LAG, GROUPS):
        out_part(g)

    pos = pos0 + t * t_len + lax.broadcasted_iota(jnp.int32, (t_len, 1), 0)
    pooled = []
    for c in range(n_uslab):
        k = POOL_WINDOWS[c * LANE // POOL_GROUP_DIM]
        wsum = upad[c, POOL_PAD - (k - 1):POOL_PAD - (k - 1) + t_len, :]
        for j in range(k - 2, -1, -1):
            wsum = wsum + upad[c, POOL_PAD - j:POOL_PAD - j + t_len, :]
        cnt = jnp.minimum(pos + 1, k).astype(F32)
        pooled.append((wsum / cnt - upad[c, POOL_PAD:POOL_PAD + t_len, :]).astype(BF16))
    slabs_per_group = POOL_GROUP_DIM // LANE
    mixed = jnp.concatenate(
        [_dot(jnp.concatenate(pooled[gi * slabs_per_group:(gi + 1) * slabs_per_group], axis=-1), mix_ref[gi])
         for gi in range(len(POOL_WINDOWS))], axis=-1)
    mixed = (mixed * pscale_ref[...]).astype(BF16)
    y_b = jnp.concatenate([_dot(mixed, w_ref[...]) for w_ref in wpp_refs], axis=-1)

    merged = (jax.nn.sigmoid(ga_s[...]) * ya_s[...] + jax.nn.sigmoid(gb_s[...]) * y_b).astype(BF16)
    m = jnp.concatenate([_dot(merged, w_ref[...]) for w_ref in wout_refs], axis=-1)
    o_ref[...] = x_ref[...] + _rms(m, post_ref[...])

    @pl.when(t == nt - 1)
    def _():
        for c in range(n_xslab):
            nconv_ref[:, lanes(c)] = xpad[c, CONV_PAD + valid - (CONV_K - 1):CONV_PAD + valid, :]
        for c in range(n_uslab):
            npool_ref[:, lanes(c)] = upad[c, POOL_PAD + valid - POOL_BUF:POOL_PAD + valid, :]
        for g in range(GROUPS):
            nssm_ref[g * GROUP_W:(g + 1) * GROUP_W, :] = hst[g].T

    @pl.when(t < nt - 1)
    def _():
        for c in range(n_xslab):
            xpad[c, 0:CONV_PAD, :] = xpad[c, t_len:t_len + CONV_PAD, :]
        for c in range(n_uslab):
            upad[c, 0:POOL_PAD, :] = upad[c, t_len:t_len + POOL_PAD, :]


def _mixer(x, q, init, valid, pos0):
    b, s, _ = x.shape
    t_len = min(s, MIX_ROWS)
    nt = s // t_len
    valid = t_len if valid is None else valid
    assert s % t_len == 0 and t_len % CHUNK == 0 and (valid == t_len or nt == 1)

    def seq(w):
        return pl.BlockSpec((None, t_len, w), lambda i, t: (i, t, 0))

    def per_b(r, w):
        return pl.BlockSpec((None, r, w), lambda i, t: (i, 0, 0))

    in_specs = ([seq(D_MODEL), _const_spec((1, D_MODEL))] + [_const_spec((D_MODEL, IN_CHUNK))] * N_IN_CHUNKS +
                [_const_spec((D_MODEL, LANE)), _const_spec((CONV_K, CONV_DIM)), _const_spec((1, CONV_DIM)),
                 _const_spec((1, LANE)), _const_spec((1, LANE)), _const_spec((1, D_INNER)),
                 _const_spec((1, D_INNER)), _const_spec((LANE, D_INNER)), _const_spec((HPG * CHUNK, GROUP_W))] +
                [_const_spec((D_INNER, OUT_CHUNK))] * OUT_SPLIT +
                [_const_spec((len(POOL_WINDOWS), POOL_GROUP_DIM, POOL_GROUP_DIM)), _const_spec((1, D_POOL))] +
                [_const_spec((D_POOL, OUT_CHUNK))] * OUT_SPLIT + [_const_spec((D_MODEL, OUT_CHUNK))] * OUT_SPLIT +
                [_const_spec((1, D_MODEL))])
    args = ([x, q["mix_pre_g"]] + q["in_w"] +
            [q["in_wdt"], q["conv_w"], q["conv_b"], q["dtb3"], q["alog3"], q["dskip_e"], q["ssd_norm_g"], q["e3"],
             q["bdmask"]] + q["w_proj_ssd"] + [q["pool_mix"], q["pool_scale"]] + q["w_proj_pool"] + q["w_out"] +
            [q["mix_post_g"]])
    if init is not None:
        in_specs += [per_b(CONV_PAD, CONV_DIM), per_b(D_INNER, STATE), per_b(POOL_PAD, D_POOL)]
        args += list(init)
    return pl.pallas_call(
        functools.partial(_mixer_kernel, t_len=t_len, nt=nt, valid=valid, pos0=pos0, has_init=init is not None),
        grid=(b, nt),
        in_specs=in_specs,
        out_specs=[seq(D_MODEL), per_b(CONV_K - 1, CONV_DIM), per_b(D_INNER, STATE), per_b(POOL_BUF, D_POOL)],
        out_shape=[jax.ShapeDtypeStruct((b, s, D_MODEL), F32),
                   jax.ShapeDtypeStruct((b, CONV_K - 1, CONV_DIM), F32),
                   jax.ShapeDtypeStruct((b, D_INNER, STATE), F32),
                   jax.ShapeDtypeStruct((b, POOL_BUF, D_POOL), F32)],
        scratch_shapes=[pltpu.VMEM((CONV_DIM // LANE, CONV_PAD + t_len, LANE), F32),
                        pltpu.VMEM((D_POOL // LANE, POOL_PAD + t_len, LANE), F32),
                        pltpu.VMEM((t_len, CONV_DIM), F32), pltpu.VMEM((t_len, D_INNER), F32),
                        pltpu.VMEM((GROUPS, STATE, GROUP_W), F32), pltpu.VMEM((t_len, D_INNER), BF16),
                        pltpu.VMEM((t_len, D_MODEL), F32), pltpu.VMEM((t_len, D_MODEL), F32),
                        pltpu.VMEM((t_len, D_MODEL), F32)],
        compiler_params=pltpu.CompilerParams(dimension_semantics=("parallel", "arbitrary"),
                                             vmem_limit_bytes=VMEM_LIMIT),
        name="mixer",
    )(*args)


def _head_expansion():
    e = np.zeros((LANE, HEADS * HEAD_DIM), np.float32)
    for j in range(DT_REP):
        for h in range(HEADS):
            e[j * HEADS + h, h * HEAD_DIM:(h + 1) * HEAD_DIM] = 1.0
    return jnp.asarray(e, BF16)


def _head_blockdiag():
    r = np.arange(HPG * CHUNK)[:, None] // CHUNK
    c = np.arange(GROUP_W)[None, :] // HEAD_DIM
    return jnp.asarray((r == c).astype(np.float32), BF16)


def _rep_heads(v):
    return jnp.concatenate([v] * DT_REP + [jnp.zeros((LANE - DT_REP * HEADS,), F32)])[None, :]


def _prep(p):
    w_in = p["w_in"]
    cuts = np.cumsum([0, D_INNER, CONV_DIM, HEADS, D_POOL, D_MODEL, D_MODEL]).tolist()
    wdt = w_in[:, cuts[2]:cuts[3]]
    wdt3 = jnp.concatenate([wdt] * DT_REP + [jnp.zeros((D_MODEL, LANE - DT_REP * HEADS), F32)], axis=1)
    skip = [HEADS if lo >= IN_CUTS[2] else 0 for lo, _ in _in_col_ranges()]
    w_main = jnp.concatenate([w_in[:, lo + d:hi + d] for (lo, hi), d in zip(_in_col_ranges(), skip)],
                             axis=1).astype(BF16)
    q = {k: v for k, v in p.items()}
    for k in ("ffn1_w_gate", "ffn1_w_up", "ffn1_w_down", "ffn2_w_gate", "ffn2_w_up", "ffn2_w_down", "pool_mix"):
        q[k] = p[k].astype(BF16)
    for k in ("w_proj_ssd", "w_proj_pool", "w_out"):
        q[k] = [p[k][:, j * OUT_CHUNK:(j + 1) * OUT_CHUNK].astype(BF16) for j in range(OUT_SPLIT)]
    for k in ("ffn1_pre_g", "ffn1_post_g", "mix_pre_g", "mix_post_g", "ffn2_pre_g", "ffn2_post_g", "conv_b",
              "ssd_norm_g", "pool_scale"):
        q[k] = p[k][None, :]
    q["in_w"] = [w_main[:, i * IN_CHUNK:(i + 1) * IN_CHUNK] for i in range(N_IN_CHUNKS)]
    q["in_wdt"] = wdt3.astype(BF16)
    q["dtb3"] = _rep_heads(p["dt_bias"])
    q["alog3"] = _rep_heads(p["a_log"])
    q["dskip_e"] = jnp.repeat(p["d_skip"], HEAD_DIM)[None, :]
    q["e3"] = _head_expansion()
    q["bdmask"] = _head_blockdiag()
    return q


def _layer(x, init, valid, pos0, q):
    b, s, _ = x.shape
    n = b * s
    tm = min(n, FFN_ROWS)
    xf = x.reshape(n, D_MODEL)
    xf = _ffn(xf, q["ffn1_pre_g"], q["ffn1_w_gate"], q["ffn1_w_up"], q["ffn1_w_down"], q["ffn1_post_g"], tm)
    xm, nconv, nssm, npool = _mixer(xf.reshape(b, s, D_MODEL), q, init, valid, pos0)
    xf = xm.reshape(n, D_MODEL)
    xf = _ffn(xf, q["ffn2_pre_g"], q["ffn2_w_gate"], q["ffn2_w_up"], q["ffn2_w_down"], q["ffn2_post_g"], tm)
    return xf.reshape(b, s, D_MODEL), nconv, nssm.reshape(b, HEADS, HEAD_DIM, STATE), npool


def kernel(x_prompt, x_sample, cache_conv, state_ssm, cache_pool, ffn1_pre_g, ffn1_post_g, ffn1_w_gate, ffn1_w_up, ffn1_w_down, mix_pre_g, mix_post_g, w_in, conv_w, conv_b, dt_bias, a_log, d_skip, ssd_norm_g, w_proj_ssd, pool_mix, pool_scale, w_proj_pool, w_out, ffn2_pre_g, ffn2_post_g, ffn2_w_gate, ffn2_w_up, ffn2_w_down):
    names = ("ffn1_pre_g", "ffn1_post_g", "ffn1_w_gate", "ffn1_w_up", "ffn1_w_down", "mix_pre_g", "mix_post_g",
             "w_in", "conv_w", "conv_b", "dt_bias", "a_log", "d_skip", "ssd_norm_g", "w_proj_ssd", "pool_mix",
             "pool_scale", "w_proj_pool", "w_out", "ffn2_pre_g", "ffn2_post_g", "ffn2_w_gate", "ffn2_w_up",
             "ffn2_w_down")
    stacked = (ffn1_pre_g, ffn1_post_g, ffn1_w_gate, ffn1_w_up, ffn1_w_down, mix_pre_g, mix_post_g, w_in, conv_w,
               conv_b, dt_bias, a_log, d_skip, ssd_norm_g, w_proj_ssd, pool_mix, pool_scale, w_proj_pool, w_out,
               ffn2_pre_g, ffn2_post_g, ffn2_w_gate, ffn2_w_up, ffn2_w_down)
    depth = w_in.shape[0]
    dec_b, dec_s, _ = x_sample.shape
    assert dec_s <= CHUNK
    y_p = x_prompt
    y_s = jnp.pad(x_sample, ((0, 0), (0, CHUNK - dec_s), (0, 0)))
    outs = [[] for _ in range(6)]
    for i in range(depth):
        q = _prep({k: v[i] for k, v in zip(names, stacked)})
        y_p, c1, s1, q1 = _layer(y_p, None, None, 0, q)
        init = (jnp.pad(cache_conv[i], ((0, 0), (CONV_PAD - (CONV_K - 1), 0), (0, 0))),
                state_ssm[i].reshape(dec_b, D_INNER, STATE),
                jnp.pad(cache_pool[i], ((0, 0), (POOL_PAD - POOL_BUF, 0), (0, 0))))
        y_s, c2, s2, q2 = _layer(y_s, init, dec_s, PAST_LEN, q)
        for lst, v in zip(outs, (c1, s1, q1, c2, s2, q2)):
            lst.append(v)
    return (y_p, y_s[:, :dec_s]) + tuple(jnp.stack(v) for v in outs)
```

```python
import functools

import numpy as np
import jax
import jax.numpy as jnp
from jax import lax
from jax.experimental import pallas as pl
from jax.experimental.pallas import tpu as pltpu

F32 = jnp.float32
BF16 = jnp.bfloat16

D_MODEL = 1024
D_FF = 2816
D_INNER = 2048
HEADS = 32
HEAD_DIM = 64
GROUPS = 8
HPG = HEADS // GROUPS
GROUP_W = HPG * HEAD_DIM
STATE = 128
CONV_K = 4
CONV_DIM = D_INNER + 2 * GROUPS * STATE
D_POOL = 1024
POOL_WINDOWS = (2, 4, 8, 16)
POOL_GROUP_DIM = D_POOL // len(POOL_WINDOWS)
POOL_BUF = max(POOL_WINDOWS) - 1
PAST_LEN = 4096
EPS = 1e-6
CHUNK = 64

LANE = 128
DT_REP = 3
CONV_PAD = 8
POOL_PAD = 16
IN_CUTS = (0, D_INNER, D_INNER + CONV_DIM, D_INNER + CONV_DIM + D_POOL, D_INNER + CONV_DIM + D_POOL + D_MODEL,
           D_INNER + CONV_DIM + D_POOL + 2 * D_MODEL)
IN_CHUNK = 1536
N_IN_CHUNKS = IN_CUTS[-1] // IN_CHUNK
GROUP_COLS = 2 * GROUP_W + 2 * STATE
GROUPS_PER_CHUNK = IN_CHUNK // GROUP_COLS
FFN_ROWS = 1024
FFN_BLOCK = 256
OUT_CHUNK = 512
OUT_SPLIT = D_MODEL // OUT_CHUNK
OUT_LAG = 2
MIX_ROWS = 256
VMEM_LIMIT = 56 * 1024 * 1024


def _const_spec(shape):
    nd = len(shape)
    return pl.BlockSpec(shape, lambda *_: (0,) * nd, pipeline_mode=pl.Buffered(1))


def _rms(x, g):
    ms = jnp.mean(x * x, axis=-1, keepdims=True)
    return x * lax.rsqrt(ms + EPS) * g


def _dot(a, b):
    return jnp.dot(a, b, preferred_element_type=F32)


def _silu(x):
    return x * jax.nn.sigmoid(x)


def _softplus(v):
    e = jnp.exp(-jnp.abs(v))
    u = 1.0 + e
    tiny = u == 1.0
    return jnp.maximum(v, 0.0) + jnp.where(tiny, e, jnp.log(u) * e / jnp.where(tiny, 1.0, u - 1.0))


def _ffn_kernel(x_ref, pre_ref, wg_ref, wu_ref, wd_ref, post_ref, o_ref):
    for r in range(x_ref.shape[0] // FFN_BLOCK):
        rows = slice(r * FFN_BLOCK, (r + 1) * FFN_BLOCK)
        x = x_ref[rows, :]
        h = _rms(x, pre_ref[...]).astype(BF16)
        g = _dot(h, wg_ref[...])
        u = _dot(h, wu_ref[...])
        a = (_silu(g) * u).astype(BF16)
        f = _dot(a, wd_ref[...])
        o_ref[rows, :] = x + 0.5 * _rms(f, post_ref[...])


def _ffn(x, pre_g, wg, wu, wd, post_g, tm):
    n = x.shape[0]
    assert n % tm == 0 and tm % FFN_BLOCK == 0
    row = pl.BlockSpec((tm, D_MODEL), lambda i: (i, 0))
    return pl.pallas_call(
        _ffn_kernel,
        grid=(n // tm,),
        in_specs=[row, _const_spec((1, D_MODEL)), _const_spec((D_MODEL, D_FF)), _const_spec((D_MODEL, D_FF)),
                  _const_spec((D_FF, D_MODEL)), _const_spec((1, D_MODEL))],
        out_specs=row,
        out_shape=jax.ShapeDtypeStruct((n, D_MODEL), F32),
        compiler_params=pltpu.CompilerParams(dimension_semantics=("parallel",), vmem_limit_bytes=VMEM_LIMIT),
        name="ffn",
    )(x, pre_g, wg, wu, wd, post_g)


def _expand_heads(v3, e3):
    return _dot(_pack_parts(v3), e3)


def _pack_parts(v3):
    hi = v3.astype(BF16).astype(F32)
    r1 = v3 - hi
    mid = r1.astype(BF16).astype(F32)
    lo = r1 - mid
    lane = lax.broadcasted_iota(jnp.int32, v3.shape, 1)
    packed = jnp.where(lane < HEADS, hi, jnp.where(lane < 2 * HEADS, mid, jnp.where(lane < 3 * HEADS, lo, 0.0)))
    return packed.astype(BF16)


def _in_col_ranges():
    ranges = []
    for g in range(GROUPS):
        ranges.append((g * GROUP_W, (g + 1) * GROUP_W))
        ranges.append((IN_CUTS[1] + g * GROUP_W, IN_CUTS[1] + (g + 1) * GROUP_W))
        ranges.append((IN_CUTS[1] + D_INNER + g * STATE, IN_CUTS[1] + D_INNER + (g + 1) * STATE))
        c0 = IN_CUTS[1] + D_INNER + GROUPS * STATE
        ranges.append((c0 + g * STATE, c0 + (g + 1) * STATE))
    ranges.append((IN_CUTS[2], IN_CUTS[5]))
    return ranges


def _in_col_order():
    return np.concatenate([np.arange(lo, hi) for lo, hi in _in_col_ranges()])


def _in_col_dest(col):
    src = int(_in_col_order()[col])
    for kind, lo, hi in zip(("z", "xbc", "u", "ga", "gb"), IN_CUTS[:-1], IN_CUTS[1:]):
        if lo <= src < hi:
            return kind, src - lo
    raise ValueError(col)


def _mixer_kernel(x_ref, pre_ref, *rest, t_len, nt, valid, pos0, has_init):
    w_refs, rest = rest[:N_IN_CHUNKS], rest[N_IN_CHUNKS:]
    wdt_ref, convw_ref, convb_ref, dtb_ref, alog_ref, dskip_ref, ng_ref, e3_ref, bdmask_ref = rest[:9]
    rest = rest[9:]
    wps_refs, rest = rest[:OUT_SPLIT], rest[OUT_SPLIT:]
    mix_ref, pscale_ref = rest[:2]
    wpp_refs, rest = rest[2:2 + OUT_SPLIT], rest[2 + OUT_SPLIT:]
    wout_refs, rest = rest[:OUT_SPLIT], rest[OUT_SPLIT:]
    post_ref, rest = rest[0], rest[1:]
    if has_init:
        conv0_ref, ssm0_ref, pool0_ref = rest[:3]
        rest = rest[3:]
    o_ref, nconv_ref, nssm_ref, npool_ref, xpad, upad, xc, zs, hst, yn_s, ya_s, ga_s, gb_s, dte_s, acse_s = rest
    t = pl.program_id(1)
    n_xslab = CONV_DIM // LANE
    n_uslab = D_POOL // LANE

    def lanes(c):
        return slice(c * LANE, (c + 1) * LANE)

    @pl.when(t == 0)
    def _():
        for c in range(n_xslab):
            xpad[c, 0:CONV_PAD, :] = conv0_ref[:, lanes(c)] if has_init else jnp.zeros((CONV_PAD, LANE), F32)
        for c in range(n_uslab):
            upad[c, 0:POOL_PAD, :] = pool0_ref[:, lanes(c)] if has_init else jnp.zeros((POOL_PAD, LANE), F32)
        for g in range(GROUPS):
            hst[g] = ssm0_ref[g * GROUP_W:(g + 1) * GROUP_W, :].T if has_init else jnp.zeros((STATE, GROUP_W), F32)

    h = _rms(x_ref[...], pre_ref[...]).astype(BF16)

    v = _dot(h, wdt_ref[...]) + dtb_ref[...]
    dt3 = _softplus(v)
    if valid < t_len:
        dt3 = jnp.where(lax.broadcasted_iota(jnp.int32, (t_len, LANE), 0) < valid, dt3, 0.0)
    a3 = -jnp.exp(alog_ref[...])
    r_i = lax.broadcasted_iota(jnp.int32, (t_len, t_len), 0)
    c_i = lax.broadcasted_iota(jnp.int32, (t_len, t_len), 1)
    tril = ((c_i <= r_i) & (c_i // CHUNK == r_i // CHUNK)).astype(BF16)
    part = _dot(tril, _pack_parts(dt3 * a3))
    acs3 = part
    for j in range(1, LANE // HEADS):
        acs3 = acs3 + pltpu.roll(part, j * HEADS, 1)

    li = lax.broadcasted_iota(jnp.int32, (CHUNK, GROUP_W), 0)
    s_of_lane = lax.broadcasted_iota(jnp.int32, (CHUNK, GROUP_W), 1) % HEAD_DIM
    diag = li == s_of_lane
    causal = li >= s_of_lane
    bdmask = bdmask_ref[...]
    b_off = D_INNER
    c_off = D_INNER + GROUPS * STATE

    def project(ci):
        o = _dot(h, w_refs[ci][...])
        for j in range(IN_CHUNK // LANE):
            kind, off = _in_col_dest(ci * IN_CHUNK + j * LANE)
            piece = o[:, lanes(j)]
            if kind == "z":
                zs[:, off:off + LANE] = _silu(piece)
            elif kind == "xbc":
                xpad[off // LANE, CONV_PAD:CONV_PAD + t_len, :] = piece
            elif kind == "u":
                upad[off // LANE, POOL_PAD:POOL_PAD + t_len, :] = piece
            elif kind == "ga":
                ga_s[:, off:off + LANE] = piece
            else:
                gb_s[:, off:off + LANE] = piece

    def conv_slab(c):
        conv = convb_ref[:, lanes(c)]
        for k in range(CONV_K):
            start = CONV_PAD - (CONV_K - 1) + k
            conv = conv + xpad[c, start:start + t_len, :] * convw_ref[k:k + 1, lanes(c)]
        xc[:, lanes(c)] = _silu(conv)

    def scan_group(g):
        ch = slice(g * GROUP_W, (g + 1) * GROUP_W)
        h_state = hst[g]
        for c in range(t_len // CHUNK):
            rows = slice(c * CHUNK, (c + 1) * CHUNK)
            ae = acse_s[rows, ch]
            acs_row = jnp.sum(jnp.where(diag, ae, 0.0), axis=0, keepdims=True)
            decay = jnp.exp(jnp.where(causal, ae - acs_row, -jnp.inf))
            bg = xc[rows, b_off + g * STATE:b_off + (g + 1) * STATE].astype(BF16)
            cg = xc[rows, c_off + g * STATE:c_off + (g + 1) * STATE].astype(BF16)
            scores = lax.dot_general(cg, jnp.concatenate([bg] * HPG, axis=0), (((1,), (1,)), ((), ())),
                                     preferred_element_type=F32)
            xg = xc[rows, ch]
            xdt = xg * dte_s[rows, ch]
            xdt_bd = jnp.concatenate([xdt.astype(BF16)] * HPG, axis=0) * bdmask
            y_diag = _dot((scores * decay).astype(BF16), xdt_bd)
            y_off = _dot(cg, h_state.astype(BF16)) * jnp.exp(ae)
            last = ae[CHUNK - 1:CHUNK, :]
            xw = (xdt * jnp.exp(last - ae)).astype(BF16)
            h_state = jnp.exp(last) * h_state + lax.dot_general(bg, xw, (((0,), (0,)), ((), ())),
                                                                preferred_element_type=F32)
            y = (y_diag + y_off + dskip_ref[:, ch] * xg) * zs[rows, ch]
            ms = jnp.mean(y * y, axis=-1, keepdims=True)
            yn_s[rows, ch] = (y * lax.rsqrt(ms + EPS) * ng_ref[:, ch]).astype(BF16)
        hst[g] = h_state

    def out_part(g):
        ch = slice(g * GROUP_W, (g + 1) * GROUP_W)
        for j, w_ref in enumerate(wps_refs):
            part = _dot(yn_s[:, ch], w_ref[ch, :])
            cols = slice(j * OUT_CHUNK, (j + 1) * OUT_CHUNK)
            ya_s[:, cols] = part if g == 0 else ya_s[:, cols] + part

    project(0)
    e3 = e3_ref[...]
    dte_s[...] = _expand_heads(dt3, e3)
    acse_s[...] = _expand_heads(acs3, e3)
    for ci in range(1, N_IN_CHUNKS):
        project(ci)
        for g in range((ci - 1) * GROUPS_PER_CHUNK, min(ci * GROUPS_PER_CHUNK, GROUPS)):
            for c in (2 * g, 2 * g + 1, b_off // LANE + g, c_off // LANE + g):
                conv_slab(c)
            scan_group(g)
            if g >= OUT_LAG:
                out_part(g - OUT_LAG)
    for g in range(GROUPS - OUT_LAG, GROUPS):
        out_part(g)

    pos = pos0 + t * t_len + lax.broadcasted_iota(jnp.int32, (t_len, 1), 0)
    pooled = []
    for c in range(n_uslab):
        k = POOL_WINDOWS[c * LANE // POOL_GROUP_DIM]
        wsum = upad[c, POOL_PAD - (k - 1):POOL_PAD - (k - 1) + t_len, :]
        for j in range(k - 2, -1, -1):
            wsum = wsum + upad[c, POOL_PAD - j:POOL_PAD - j + t_len, :]
        cnt = jnp.minimum(pos + 1, k).astype(F32)
        pooled.append((wsum / cnt - upad[c, POOL_PAD:POOL_PAD + t_len, :]).astype(BF16))
    slabs_per_group = POOL_GROUP_DIM // LANE
    mixed = jnp.concatenate(
        [_dot(jnp.concatenate(pooled[gi * slabs_per_group:(gi + 1) * slabs_per_group], axis=-1), mix_ref[gi])
         for gi in range(len(POOL_WINDOWS))], axis=-1)
    mixed = (mixed * pscale_ref[...]).astype(BF16)
    y_b = jnp.concatenate([_dot(mixed, w_ref[...]) for w_ref in wpp_refs], axis=-1)

    merged = (jax.nn.sigmoid(ga_s[...]) * ya_s[...] + jax.nn.sigmoid(gb_s[...]) * y_b).astype(BF16)
    m = jnp.concatenate([_dot(merged, w_ref[...]) for w_ref in wout_refs], axis=-1)
    o_ref[...] = x_ref[...] + _rms(m, post_ref[...])

    @pl.when(t == nt - 1)
    def _():
        for c in range(n_xslab):
            nconv_ref[:, lanes(c)] = xpad[c, CONV_PAD + valid - (CONV_K - 1):CONV_PAD + valid, :]
        for c in range(n_uslab):
            npool_ref[:, lanes(c)] = upad[c, POOL_PAD + valid - POOL_BUF:POOL_PAD + valid, :]
        for g in range(GROUPS):
            nssm_ref[g * GROUP_W:(g + 1) * GROUP_W, :] = hst[g].T

    @pl.when(t < nt - 1)
    def _():
        for c in range(n_xslab):
            xpad[c, 0:CONV_PAD, :] = xpad[c, t_len:t_len + CONV_PAD, :]
        for c in range(n_uslab):
            upad[c, 0:POOL_PAD, :] = upad[c, t_len:t_len + POOL_PAD, :]


def _mixer(x, q, init, valid, pos0):
    b, s, _ = x.shape
    t_len = min(s, MIX_ROWS)
    nt = s // t_len
    valid = t_len if valid is None else valid
    assert s % t_len == 0 and t_len % CHUNK == 0 and (valid == t_len or nt == 1)

    def seq(w):
        return pl.BlockSpec((None, t_len, w), lambda i, t: (i, t, 0))

    def per_b(r, w):
        return pl.BlockSpec((None, r, w), lambda i, t: (i, 0, 0))

    in_specs = ([seq(D_MODEL), _const_spec((1, D_MODEL))] + [_const_spec((D_MODEL, IN_CHUNK))] * N_IN_CHUNKS +
                [_const_spec((D_MODEL, LANE)), _const_spec((CONV_K, CONV_DIM)), _const_spec((1, CONV_DIM)),
                 _const_spec((1, LANE)), _const_spec((1, LANE)), _const_spec((1, D_INNER)),
                 _const_spec((1, D_INNER)), _const_spec((LANE, D_INNER)), _const_spec((HPG * CHUNK, GROUP_W))] +
                [_const_spec((D_INNER, OUT_CHUNK))] * OUT_SPLIT +
                [_const_spec((len(POOL_WINDOWS), POOL_GROUP_DIM, POOL_GROUP_DIM)), _const_spec((1, D_POOL))] +
                [_const_spec((D_POOL, OUT_CHUNK))] * OUT_SPLIT + [_const_spec((D_MODEL, OUT_CHUNK))] * OUT_SPLIT +
                [_const_spec((1, D_MODEL))])
    args = ([x, q["mix_pre_g"]] + q["in_w"] +
            [q["in_wdt"], q["conv_w"], q["conv_b"], q["dtb3"], q["alog3"], q["dskip_e"], q["ssd_norm_g"], q["e3"],
             q["bdmask"]] + q["w_proj_ssd"] + [q["pool_mix"], q["pool_scale"]] + q["w_proj_pool"] + q["w_out"] +
            [q["mix_post_g"]])
    if init is not None:
        in_specs += [per_b(CONV_PAD, CONV_DIM), per_b(D_INNER, STATE), per_b(POOL_PAD, D_POOL)]
        args += list(init)
    return pl.pallas_call(
        functools.partial(_mixer_kernel, t_len=t_len, nt=nt, valid=valid, pos0=pos0, has_init=init is not None),
        grid=(b, nt),
        in_specs=in_specs,
        out_specs=[seq(D_MODEL), per_b(CONV_K - 1, CONV_DIM), per_b(D_INNER, STATE), per_b(POOL_BUF, D_POOL)],
        out_shape=[jax.ShapeDtypeStruct((b, s, D_MODEL), F32),
                   jax.ShapeDtypeStruct((b, CONV_K - 1, CONV_DIM), F32),
                   jax.ShapeDtypeStruct((b, D_INNER, STATE), F32),
                   jax.ShapeDtypeStruct((b, POOL_BUF, D_POOL), F32)],
        scratch_shapes=[pltpu.VMEM((CONV_DIM // LANE, CONV_PAD + t_len, LANE), F32),
                        pltpu.VMEM((D_POOL // LANE, POOL_PAD + t_len, LANE), F32),
                        pltpu.VMEM((t_len, CONV_DIM), F32), pltpu.VMEM((t_len, D_INNER), F32),
                        pltpu.VMEM((GROUPS, STATE, GROUP_W), F32), pltpu.VMEM((t_len, D_INNER), BF16),
                        pltpu.VMEM((t_len, D_MODEL), F32), pltpu.VMEM((t_len, D_MODEL), F32),
                        pltpu.VMEM((t_len, D_MODEL), F32), pltpu.VMEM((t_len, D_INNER), F32),
                        pltpu.VMEM((t_len, D_INNER), F32)],
        compiler_params=pltpu.CompilerParams(dimension_semantics=("parallel", "arbitrary"),
                                             vmem_limit_bytes=VMEM_LIMIT),
        name="mixer",
    )(*args)


def _head_expansion():
    e = np.zeros((LANE, HEADS * HEAD_DIM), np.float32)
    for j in range(DT_REP):
        for h in range(HEADS):
            e[j * HEADS + h, h * HEAD_DIM:(h + 1) * HEAD_DIM] = 1.0
    return jnp.asarray(e, BF16)


def _head_blockdiag():
    r = np.arange(HPG * CHUNK)[:, None] // CHUNK
    c = np.arange(GROUP_W)[None, :] // HEAD_DIM
    return jnp.asarray((r == c).astype(np.float32), BF16)


def _rep_heads(v):
    return jnp.concatenate([v] * DT_REP + [jnp.zeros((LANE - DT_REP * HEADS,), F32)])[None, :]


def _prep(p):
    w_in = p["w_in"]
    cuts = np.cumsum([0, D_INNER, CONV_DIM, HEADS, D_POOL, D_MODEL, D_MODEL]).tolist()
    wdt = w_in[:, cuts[2]:cuts[3]]
    wdt3 = jnp.concatenate([wdt] * DT_REP + [jnp.zeros((D_MODEL, LANE - DT_REP * HEADS), F32)], axis=1)
    skip = [HEADS if lo >= IN_CUTS[2] else 0 for lo, _ in _in_col_ranges()]
    w_main = jnp.concatenate([w_in[:, lo + d:hi + d] for (lo, hi), d in zip(_in_col_ranges(), skip)],
                             axis=1).astype(BF16)
    q = {k: v for k, v in p.items()}
    for k in ("ffn1_w_gate", "ffn1_w_up", "ffn1_w_down", "ffn2_w_gate", "ffn2_w_up", "ffn2_w_down", "pool_mix"):
        q[k] = p[k].astype(BF16)
    for k in ("w_proj_ssd", "w_proj_pool", "w_out"):
        q[k] = [p[k][:, j * OUT_CHUNK:(j + 1) * OUT_CHUNK].astype(BF16) for j in range(OUT_SPLIT)]
    for k in ("ffn1_pre_g", "ffn1_post_g", "mix_pre_g", "mix_post_g", "ffn2_pre_g", "ffn2_post_g", "conv_b",
              "ssd_norm_g", "pool_scale"):
        q[k] = p[k][None, :]
    q["in_w"] = [w_main[:, i * IN_CHUNK:(i + 1) * IN_CHUNK] for i in range(N_IN_CHUNKS)]
    q["in_wdt"] = wdt3.astype(BF16)
    q["dtb3"] = _rep_heads(p["dt_bias"])
    q["alog3"] = _rep_heads(p["a_log"])
    q["dskip_e"] = jnp.repeat(p["d_skip"], HEAD_DIM)[None, :]
    q["e3"] = _head_expansion()
    q["bdmask"] = _head_blockdiag()
    return q


def _layer(x, init, valid, pos0, q):
    b, s, _ = x.shape
    n = b * s
    tm = min(n, FFN_ROWS)
    xf = x.reshape(n, D_MODEL)
    xf = _ffn(xf, q["ffn1_pre_g"], q["ffn1_w_gate"], q["ffn1_w_up"], q["ffn1_w_down"], q["ffn1_post_g"], tm)
    xm, nconv, nssm, npool = _mixer(xf.reshape(b, s, D_MODEL), q, init, valid, pos0)
    xf = xm.reshape(n, D_MODEL)
    xf = _ffn(xf, q["ffn2_pre_g"], q["ffn2_w_gate"], q["ffn2_w_up"], q["ffn2_w_down"], q["ffn2_post_g"], tm)
    return xf.reshape(b, s, D_MODEL), nconv, nssm.reshape(b, HEADS, HEAD_DIM, STATE), npool


def kernel(x_prompt, x_sample, cache_conv, state_ssm, cache_pool, ffn1_pre_g, ffn1_post_g, ffn1_w_gate, ffn1_w_up, ffn1_w_down, mix_pre_g, mix_post_g, w_in, conv_w, conv_b, dt_bias, a_log, d_skip, ssd_norm_g, w_proj_ssd, pool_mix, pool_scale, w_proj_pool, w_out, ffn2_pre_g, ffn2_post_g, ffn2_w_gate, ffn2_w_up, ffn2_w_down):
    names = ("ffn1_pre_g", "ffn1_post_g", "ffn1_w_gate", "ffn1_w_up", "ffn1_w_down", "mix_pre_g", "mix_post_g",
             "w_in", "conv_w", "conv_b", "dt_bias", "a_log", "d_skip", "ssd_norm_g", "w_proj_ssd", "pool_mix",
             "pool_scale", "w_proj_pool", "w_out", "ffn2_pre_g", "ffn2_post_g", "ffn2_w_gate", "ffn2_w_up",
             "ffn2_w_down")
    stacked = (ffn1_pre_g, ffn1_post_g, ffn1_w_gate, ffn1_w_up, ffn1_w_down, mix_pre_g, mix_post_g, w_in, conv_w,
               conv_b, dt_bias, a_log, d_skip, ssd_norm_g, w_proj_ssd, pool_mix, pool_scale, w_proj_pool, w_out,
               ffn2_pre_g, ffn2_post_g, ffn2_w_gate, ffn2_w_up, ffn2_w_down)
    depth = w_in.shape[0]
    dec_b, dec_s, _ = x_sample.shape
    assert dec_s <= CHUNK
    y_p = x_prompt
    y_s = jnp.pad(x_sample, ((0, 0), (0, CHUNK - dec_s), (0, 0)))
    outs = [[] for _ in range(6)]
    for i in range(depth):
        q = _prep({k: v[i] for k, v in zip(names, stacked)})
        y_p, c1, s1, q1 = _layer(y_p, None, None, 0, q)
        init = (jnp.pad(cache_conv[i], ((0, 0), (CONV_PAD - (CONV_K - 1), 0), (0, 0))),
                state_ssm[i].reshape(dec_b, D_INNER, STATE),
                jnp.pad(cache_pool[i], ((0, 0), (POOL_PAD - POOL_BUF, 0), (0, 0))))
        y_s, c2, s2, q2 = _layer(y_s, init, dec_s, PAST_LEN, q)
        for lst, v in zip(outs, (c1, s1, q1, c2, s2, q2)):
            lst.append(v)
    return (y_p, y_s[:, :dec_s]) + tuple(jnp.stack(v) for v in outs)
```
